```python
import jax, jax.numpy as jnp
from jax import lax
import numpy as np

D_MODEL = 1024
BATCH = 2
SEQ = 8192
DEPTH = 2

CTX_LEN = 256
GRID_W = 64
N_MIXERS = 2
N_HEADS = 8
N_KV_HEADS = 2
HEAD_DIM = 128
GROUP = N_HEADS // N_KV_HEADS
Q_DIM = N_HEADS * HEAD_DIM
KV_DIM = N_KV_HEADS * HEAD_DIM
QKV_DIM = Q_DIM + 2 * KV_DIM
Q_BLOCK = 128
ROPE_AXIS_DIM = HEAD_DIM // 2
ROPE_THETA = 10000.0
CHUNK = 128
SGU_DIM = 3 * D_MODEL
SGU_GROUPS = 8
SGU_GROUP_DIM = SGU_DIM // SGU_GROUPS
D_FF = 4 * D_MODEL
N_MOD = 6
EPS = 1e-6

kernel_name = "hybrid_gqa_sgu_prefix_dit"


def rms_norm(x, g):
    xf = x.astype(jnp.float32)
    y = xf * lax.rsqrt(jnp.mean(xf * xf, axis=-1, keepdims=True) + EPS) * g.astype(jnp.float32)
    return y.astype(x.dtype)


def modulate(h, shift, scale):
    return h * (1 + scale[:, None, :]) + shift[:, None, :]


def grid_rope_tables(n):
    rows_count = n // GRID_W
    rows = jnp.repeat(jnp.arange(rows_count, dtype=jnp.int32), GRID_W).astype(jnp.float32)
    cols = jnp.tile(jnp.arange(GRID_W, dtype=jnp.int32), rows_count).astype(jnp.float32)
    freqs = 1.0 / (ROPE_THETA ** (jnp.arange(0, ROPE_AXIS_DIM, 2, dtype=jnp.float32) / ROPE_AXIS_DIM))
    ang = jnp.concatenate([rows[:, None] * freqs, cols[:, None] * freqs], axis=-1)
    return jnp.cos(ang), jnp.sin(ang)


def apply_rope(x, cos, sin):
    xf = x.astype(jnp.float32).reshape(x.shape[:-1] + (HEAD_DIM // 2, 2))
    x1, x2 = xf[..., 0], xf[..., 1]
    c = cos[None, :, None, :]
    s = sin[None, :, None, :]
    out = jnp.stack([x1 * c - x2 * s, x1 * s + x2 * c], axis=-1)
    return out.reshape(x.shape).astype(x.dtype)


def block_attention(q, k, v):
    s = jnp.einsum("bqhgd,bkhd->bhgqk", q, k, preferred_element_type=jnp.float32) * (HEAD_DIM ** -0.5)
    p = jax.nn.softmax(s, axis=-1).astype(v.dtype)
    return jnp.einsum("bhgqk,bkhd->bqhgd", p, v)


def gqa_attention(hx, hc, wqkv, q_g, k_g, wo, cos, sin, need_ctx_out):
    b, n, _ = hx.shape
    qkv = hx @ wqkv
    q = qkv[..., :Q_DIM].reshape(b, n, N_HEADS, HEAD_DIM)
    k = qkv[..., Q_DIM:Q_DIM + KV_DIM].reshape(b, n, N_KV_HEADS, HEAD_DIM)
    v = qkv[..., Q_DIM + KV_DIM:].reshape(b, n, N_KV_HEADS, HEAD_DIM)
    q = apply_rope(rms_norm(q, q_g), cos, sin)
    k = apply_rope(rms_norm(k, k_g), cos, sin)
    kvc = hc @ wqkv[:, Q_DIM:]
    kc = rms_norm(kvc[..., :KV_DIM].reshape(b, -1, N_KV_HEADS, HEAD_DIM), k_g)
    vc = kvc[..., KV_DIM:].reshape(b, -1, N_KV_HEADS, HEAD_DIM)
    k_all = jnp.concatenate([kc, k], axis=1)
    v_all = jnp.concatenate([vc, v], axis=1)
    qb = q.reshape(b, n // Q_BLOCK, Q_BLOCK, N_KV_HEADS, GROUP, HEAD_DIM)
    qb = jnp.moveaxis(qb, 1, 0)
    ob = lax.map(lambda qi: block_attention(qi, k_all, v_all), qb)
    ox = jnp.moveaxis(ob, 0, 1).reshape(b, n, Q_DIM)
    yx = ox @ wo
    yc = None
    if need_ctx_out:
        lc = hc.shape[1]
        qc = rms_norm((hc @ wqkv[:, :Q_DIM]).reshape(b, lc, N_HEADS, HEAD_DIM), q_g)
        qc = qc.reshape(b, lc, N_KV_HEADS, GROUP, HEAD_DIM)
        oc = block_attention(qc, kc, vc).reshape(b, lc, Q_DIM)
        yc = oc @ wo
    return yx, yc


def chunked_sgu(h, w_in, b_in, v_g, w_s, b_s, w_out):
    b, n, _ = h.shape
    z = jax.nn.gelu(h @ w_in + b_in, approximate=False)
    u = z[..., :SGU_DIM]
    v = rms_norm(z[..., SGU_DIM:], v_g)
    v = v.reshape(b, n // CHUNK, CHUNK, SGU_GROUPS, SGU_GROUP_DIM)
    sv = jnp.einsum("gpq,bcqgd->bcpgd", w_s, v) + jnp.transpose(b_s)[None, None, :, :, None]
    return (u * sv.reshape(b, n, SGU_DIM)) @ w_out


def sq_relu_mlp(h, w1, w2):
    a = jax.nn.relu(h @ w1)
    return (a * a) @ w2


def setup_inputs(seed: int = 0) -> dict:
    key = jax.random.key(seed)
    ks = jax.random.split(key, 24)
    n_attn = (DEPTH + N_MIXERS - 1) // N_MIXERS
    n_sgu = DEPTH // N_MIXERS
    nrm = jax.random.normal
    f32 = jnp.float32
    return {
        "x": nrm(ks[0], (BATCH, SEQ, D_MODEL), f32),
        "c": nrm(ks[1], (BATCH, D_MODEL), f32),
        "ctx": nrm(ks[2], (BATCH, CTX_LEN, D_MODEL), f32),
        "c_ctx": nrm(ks[3], (D_MODEL,), f32),
        "ada_w": nrm(ks[4], (DEPTH, D_MODEL, N_MOD * D_MODEL), f32) * (0.5 * D_MODEL ** -0.5),
        "ada_b": nrm(ks[5], (DEPTH, N_MOD * D_MODEL), f32) * 0.01,
        "mix_norm_g": 1.0 + 0.02 * nrm(ks[6], (DEPTH, D_MODEL), f32),
        "mlp_norm_g": 1.0 + 0.02 * nrm(ks[7], (DEPTH, D_MODEL), f32),
        "mlp_w1": nrm(ks[8], (DEPTH, D_MODEL, D_FF), f32) * D_MODEL ** -0.5,
        "mlp_w2": nrm(ks[9], (DEPTH, D_FF, D_MODEL), f32) * D_FF ** -0.5,
        "attn_wqkv": nrm(ks[10], (n_attn, D_MODEL, QKV_DIM), f32) * D_MODEL ** -0.5,
        "attn_q_g": 1.0 + 0.02 * nrm(ks[11], (n_attn, HEAD_DIM), f32),
        "attn_k_g": 1.0 + 0.02 * nrm(ks[12], (n_attn, HEAD_DIM), f32),
        "attn_wo": nrm(ks[13], (n_attn, Q_DIM, D_MODEL), f32) * Q_DIM ** -0.5,
        "sgu_w_in": nrm(ks[14], (n_sgu, D_MODEL, 2 * SGU_DIM), f32) * D_MODEL ** -0.5,
        "sgu_b_in": nrm(ks[15], (n_sgu, 2 * SGU_DIM), f32) * 0.01,
        "sgu_v_g": 1.0 + 0.02 * nrm(ks[16], (n_sgu, SGU_DIM), f32),
        "sgu_w_s": nrm(ks[17], (n_sgu, SGU_GROUPS, CHUNK, CHUNK), f32) * CHUNK ** -0.5,
        "sgu_b_s": 1.0 + 0.01 * nrm(ks[18], (n_sgu, SGU_GROUPS, CHUNK), f32),
        "sgu_w_out": nrm(ks[19], (n_sgu, SGU_DIM, D_MODEL), f32) * SGU_DIM ** -0.5,
        "final_g": 1.0 + 0.02 * nrm(ks[20], (D_MODEL,), f32),
    }


def reference(x, c, ctx, c_ctx, ada_w, ada_b, mix_norm_g, mlp_norm_g, mlp_w1, mlp_w2,
              attn_wqkv, attn_q_g, attn_k_g, attn_wo,
              sgu_w_in, sgu_b_in, sgu_v_g, sgu_w_s, sgu_b_s, sgu_w_out, final_g):
    n = x.shape[1]
    cos, sin = grid_rope_tables(n)
    sc = jax.nn.silu(c)
    scc = jax.nn.silu(c_ctx)[None, :]
    for i in range(DEPTH):
        last = i == DEPTH - 1
        use_attn = (i % N_MIXERS) == 0
        j = i // N_MIXERS
        mx = jnp.split(sc @ ada_w[i] + ada_b[i], N_MOD, axis=-1)
        mc = jnp.split(scc @ ada_w[i] + ada_b[i], N_MOD, axis=-1)
        hx = modulate(rms_norm(x, mix_norm_g[i]), mx[0], mx[1])
        hc = None
        if use_attn or not last:
            hc = modulate(rms_norm(ctx, mix_norm_g[i]), mc[0], mc[1])
        if use_attn:
            yx, yc = gqa_attention(hx, hc, attn_wqkv[j], attn_q_g[j], attn_k_g[j], attn_wo[j],
                                   cos, sin, not last)
        else:
            yx = chunked_sgu(hx, sgu_w_in[j], sgu_b_in[j], sgu_v_g[j], sgu_w_s[j], sgu_b_s[j], sgu_w_out[j])
            yc = None
            if not last:
                yc = chunked_sgu(hc, sgu_w_in[j], sgu_b_in[j], sgu_v_g[j], sgu_w_s[j], sgu_b_s[j], sgu_w_out[j])
        x = x + mx[2][:, None, :] * yx
        x = x + mx[5][:, None, :] * sq_relu_mlp(
            modulate(rms_norm(x, mlp_norm_g[i]), mx[3], mx[4]), mlp_w1[i], mlp_w2[i])
        if not last:
            ctx = ctx + mc[2][:, None, :] * yc
            ctx = ctx + mc[5][:, None, :] * sq_relu_mlp(
                modulate(rms_norm(ctx, mlp_norm_g[i]), mc[3], mc[4]), mlp_w1[i], mlp_w2[i])
    return rms_norm(x, final_g)
```

```python
import functools

import jax
import jax.numpy as jnp
import numpy as np
from jax import lax
from jax.experimental import pallas as pl
from jax.experimental.pallas import tpu as pltpu

D_MODEL = 1024
DEPTH = 2
GRID_W = 64
N_MIXERS = 2
N_HEADS = 8
N_KV_HEADS = 2
HEAD_DIM = 128
GROUP = N_HEADS // N_KV_HEADS
Q_DIM = N_HEADS * HEAD_DIM
KV_DIM = N_KV_HEADS * HEAD_DIM
QKV_DIM = Q_DIM + 2 * KV_DIM
ROPE_AXIS_DIM = HEAD_DIM // 2
ROPE_THETA = 10000.0
CHUNK = 128
SGU_DIM = 3 * D_MODEL
SGU_GROUPS = 8
SGU_GROUP_DIM = SGU_DIM // SGU_GROUPS
D_FF = 4 * D_MODEL
N_MOD = 6
EPS = 1e-6

F32 = jnp.float32
BF16 = jnp.bfloat16

VMEM_LIMIT_BYTES = 56 * 1024 * 1024
MOD_ROWS = 8

SOFTMAX_Q_SCALE = float(HEAD_DIM ** -0.5 * np.log2(np.e))
NEG_BIG = -1e30


def _params(n_axes):
    return pltpu.CompilerParams(dimension_semantics=("parallel",) * n_axes,
                                vmem_limit_bytes=VMEM_LIMIT_BYTES)


def _resident(shape, index_map):
    return pl.BlockSpec(shape, index_map, pipeline_mode=pl.Buffered(1))


def _rms_mod(x, g, shift, scale):
    y = x * lax.rsqrt(jnp.mean(x * x, axis=-1, keepdims=True) + EPS) * g
    return y * (1.0 + scale) + shift


def _mod_spec(k, tiles_per_row):
    return pl.BlockSpec((None, 1, D_MODEL), lambda i: (i // tiles_per_row, 0, k))


def _adaln_kernel(c_ref, w_ref, b_ref, o_ref):
    c = c_ref[...]
    s = c * jax.nn.sigmoid(c)
    o_ref[...] = jnp.dot(s.astype(BF16), w_ref[...].astype(BF16),
                         preferred_element_type=F32) + b_ref[...]


def _adaln(cond, ada_w, ada_b):
    depth = ada_w.shape[0]
    return pl.pallas_call(
        _adaln_kernel,
        grid=(depth, N_MOD),
        in_specs=[
            pl.BlockSpec((MOD_ROWS, D_MODEL), lambda i, j: (0, 0)),
            pl.BlockSpec((None, D_MODEL, D_MODEL), lambda i, j: (i, 0, j)),
            pl.BlockSpec((None, 1, D_MODEL), lambda i, j: (i, 0, j)),
        ],
        out_specs=pl.BlockSpec((None, MOD_ROWS, D_MODEL), lambda i, j: (i, 0, j)),
        out_shape=jax.ShapeDtypeStruct((depth, MOD_ROWS, N_MOD * D_MODEL), F32),
        compiler_params=_params(2),
        name="adaln",
    )(cond, ada_w, ada_b.reshape(depth, 1, N_MOD * D_MODEL))


def _cos_sin_kernel(ang_ref, cos_ref, sin_ref):
    a = ang_ref[...]
    cos_ref[...] = jnp.cos(a)
    sin_ref[...] = jnp.sin(a)


def _rope_tables(n):
    rows_count = n // GRID_W
    freqs = 1.0 / (ROPE_THETA ** (jnp.arange(0, ROPE_AXIS_DIM, 2, dtype=F32) / ROPE_AXIS_DIM))
    pos = jnp.arange(rows_count + GRID_W, dtype=jnp.int32)
    pos = jnp.where(pos < rows_count, pos, pos - rows_count).astype(F32)
    ang = pos[:, None] * freqs[None, :]
    cos_s, sin_s = pl.pallas_call(
        _cos_sin_kernel,
        out_shape=(jax.ShapeDtypeStruct(ang.shape, F32),) * 2,
        name="rope_cos_sin",
    )(ang)

    def expand(t):
        by_row = jnp.repeat(t[:rows_count], GRID_W, axis=0)
        by_col = jnp.tile(t[rows_count:], (rows_count, 1))
        return jnp.concatenate([by_row, by_col], axis=-1)

    c, s = expand(cos_s), expand(sin_s)
    return jnp.concatenate([c, c], axis=-1), jnp.concatenate([-s, s], axis=-1)


def _head_norm_rope(t, g, cos, sin):
    t = t * lax.rsqrt(jnp.mean(t * t, axis=-1, keepdims=True) + EPS) * g
    return t * cos + pltpu.roll(t, HEAD_DIM // 2, axis=1) * sin


def _qkv_kernel(x_ref, g_ref, shift_ref, scale_ref, w_ref, qg_ref, kg_ref, cos_ref, sin_ref,
                q_ref, k_ref, vt_ref):
    h = _rms_mod(x_ref[...], g_ref[...], shift_ref[...], scale_ref[...]).astype(BF16)
    cos = cos_ref[...]
    sin = sin_ref[...]
    q = jnp.dot(h, w_ref[:, :Q_DIM], preferred_element_type=F32)
    qg = qg_ref[...]
    for hd in range(N_HEADS):
        sl = slice(hd * HEAD_DIM, (hd + 1) * HEAD_DIM)
        qh = _head_norm_rope(q[:, sl], qg, cos, sin)
        q_ref[:, sl] = (qh * SOFTMAX_Q_SCALE).astype(BF16)
    kv = jnp.dot(h, w_ref[:, Q_DIM:], preferred_element_type=F32)
    kg = kg_ref[...]
    for hd in range(N_KV_HEADS):
        sl = slice(hd * HEAD_DIM, (hd + 1) * HEAD_DIM)
        k_ref[hd] = _head_norm_rope(kv[:, sl], kg, cos, sin).astype(BF16)
        v = kv[:, KV_DIM + hd * HEAD_DIM:KV_DIM + (hd + 1) * HEAD_DIM]
        vt_ref[hd] = v.T.astype(BF16)


def _qkv(x2d, seq, tm, mod, mod_tiles_per_row, norm_g, wqkv, q_g, k_g, cos, sin):
    t = x2d.shape[0]
    tpb = seq // tm
    const = lambda i: (0, 0)
    return pl.pallas_call(
        _qkv_kernel,
        grid=(t // tm,),
        in_specs=[
            pl.BlockSpec((tm, D_MODEL), lambda i: (i, 0)),
            pl.BlockSpec((1, D_MODEL), const),
            _mod_spec(0, mod_tiles_per_row),
            _mod_spec(1, mod_tiles_per_row),
            _resident((D_MODEL, QKV_DIM), const),
            pl.BlockSpec((1, HEAD_DIM), const),
            pl.BlockSpec((1, HEAD_DIM), const),
            pl.BlockSpec((tm, HEAD_DIM), lambda i: (i % tpb, 0)),
            pl.BlockSpec((tm, HEAD_DIM), lambda i: (i % tpb, 0)),
        ],
        out_specs=[
            pl.BlockSpec((tm, Q_DIM), lambda i: (i, 0)),
            pl.BlockSpec((None, N_KV_HEADS, tm, HEAD_DIM), lambda i: (i // tpb, 0, i % tpb, 0)),
            pl.BlockSpec((None, N_KV_HEADS, HEAD_DIM, tm), lambda i: (i // tpb, 0, 0, i % tpb)),
        ],
        out_shape=[
            jax.ShapeDtypeStruct((t, Q_DIM), BF16),
            jax.ShapeDtypeStruct((t // seq, N_KV_HEADS, seq, HEAD_DIM), BF16),
            jax.ShapeDtypeStruct((t // seq, N_KV_HEADS, HEAD_DIM, seq), BF16),
        ],
        compiler_params=_params(1),
        name="qkv_proj",
    )(x2d, norm_g, mod, mod, wqkv, q_g, k_g, cos, sin)


def _flash_kernel(q_ref, k_ref, vt_ref, o_ref, *, tk, nk):
    tq = q_ref.shape[0]
    for g in range(GROUP):
        sl = slice(g * HEAD_DIM, (g + 1) * HEAD_DIM)
        q = q_ref[:, sl]

        def body(j, carry, q=q):
            m, l, acc = carry
            ks = pl.multiple_of(j * tk, tk)
            s = lax.dot_general(k_ref[pl.ds(ks, tk), :], q, (((1,), (1,)), ((), ())),
                                preferred_element_type=F32)
            m_new = jnp.maximum(m, jnp.max(s, axis=0, keepdims=True))
            alpha = jnp.exp2(m - m_new)
            p = jnp.exp2(s - m_new)
            l = alpha * l + jnp.sum(p, axis=0, keepdims=True)
            pv = jnp.dot(vt_ref[:, pl.ds(ks, tk)], p.astype(BF16),
                         preferred_element_type=F32)
            return m_new, l, alpha * acc + pv

        init = (jnp.full((1, tq), NEG_BIG, F32), jnp.zeros((1, tq), F32),
                jnp.zeros((HEAD_DIM, tq), F32))
        _, l, acc = lax.fori_loop(0, nk, body, init)
        o_ref[:, sl] = (acc * (1.0 / l)).T.astype(BF16)


def _flash(q, k, vt, seq_q, tq, tk):
    b, _, lk, _ = k.shape
    nq = seq_q // tq
    gw = GROUP * HEAD_DIM
    return pl.pallas_call(
        functools.partial(_flash_kernel, tk=tk, nk=lk // tk),
        grid=(b, N_KV_HEADS, nq),
        in_specs=[
            pl.BlockSpec((tq, gw), lambda bi, h, i: (bi * nq + i, h)),
            pl.BlockSpec((None, None, lk, HEAD_DIM), lambda bi, h, i: (bi, h, 0, 0)),
            pl.BlockSpec((None, None, HEAD_DIM, lk), lambda bi, h, i: (bi, h, 0, 0)),
        ],
        out_specs=pl.BlockSpec((tq, gw), lambda bi, h, i: (bi * nq + i, h)),
        out_shape=jax.ShapeDtypeStruct(q.shape, BF16),
        compiler_params=_params(3),
        name="flash_attn",
    )(q, k, vt)


def _mlp_kernel(*refs, with_proj, with_final_norm, ff_chunk):
    refs = list(refs)
    x_ref = refs.pop(0)
    if with_proj:
        o_ref, wo_ref, gate_mix_ref = refs.pop(0), refs.pop(0), refs.pop(0)
    g_ref, shift_ref, scale_ref, gate_ref, w1_ref, w2_ref = refs[:6]
    refs = refs[6:]
    if with_final_norm:
        fg_ref = refs.pop(0)
    (out_ref,) = refs

    x = x_ref[...]
    if with_proj:
        x = x + gate_mix_ref[...] * jnp.dot(o_ref[...], wo_ref[...], preferred_element_type=F32)
    h = _rms_mod(x, g_ref[...], shift_ref[...], scale_ref[...]).astype(BF16)
    y = jnp.zeros_like(x)
    for c in range(D_FF // ff_chunk):
        sl = slice(c * ff_chunk, (c + 1) * ff_chunk)
        a = jnp.maximum(jnp.dot(h, w1_ref[:, sl], preferred_element_type=F32), 0.0)
        y = y + jnp.dot((a * a).astype(BF16), w2_ref[sl, :], preferred_element_type=F32)
    x = x + gate_ref[...] * y
    if with_final_norm:
        x = x * lax.rsqrt(jnp.mean(x * x, axis=-1, keepdims=True) + EPS) * fg_ref[...]
    out_ref[...] = x


def _mlp(x2d, tm, mod, mod_tiles_per_row, norm_g, w1, w2, *, attn_o=None, wo=None, final_g=None):
    t = x2d.shape[0]
    const = lambda i: (0, 0)
    tile = pl.BlockSpec((tm, D_MODEL), lambda i: (i, 0))
    args, specs = [x2d], [tile]
    if attn_o is not None:
        args += [attn_o, wo, mod]
        specs += [pl.BlockSpec((tm, Q_DIM), lambda i: (i, 0)), _resident((Q_DIM, D_MODEL), const),
                  _mod_spec(2, mod_tiles_per_row)]
    args += [norm_g, mod, mod, mod, w1, w2]
    specs += [pl.BlockSpec((1, D_MODEL), const), _mod_spec(3, mod_tiles_per_row),
              _mod_spec(4, mod_tiles_per_row), _mod_spec(5, mod_tiles_per_row),
              _resident((D_MODEL, D_FF), const), _resident((D_FF, D_MODEL), const)]
    if final_g is not None:
        args.append(final_g)
        specs.append(pl.BlockSpec((1, D_MODEL), const))
    return pl.pallas_call(
        functools.partial(_mlp_kernel, with_proj=attn_o is not None,
                          with_final_norm=final_g is not None, ff_chunk=1024),
        grid=(t // tm,),
        in_specs=specs,
        out_specs=tile,
        out_shape=jax.ShapeDtypeStruct(x2d.shape, F32),
        compiler_params=_params(1),
        name="mlp",
    )(*args)


def _gelu(z):
    return 0.5 * z * (1.0 + lax.erf(z * float(np.sqrt(0.5))))


SGU_COL_BLOCK = 2 * SGU_GROUP_DIM


def _sgu_kernel(x_ref, g_ref, shift_ref, scale_ref, gate_ref, win_ref, bin_ref, vg_ref, ws_ref,
                bs_ref, wout_ref, out_ref, v_scr, t_scr):
    tm = x_ref.shape[0]
    x = x_ref[...]
    h = _rms_mod(x, g_ref[...], shift_ref[...], scale_ref[...]).astype(BF16)
    zv = _gelu(jnp.dot(h, win_ref[:, SGU_DIM:], preferred_element_type=F32) + bin_ref[:, SGU_DIM:])
    r = lax.rsqrt(jnp.mean(zv * zv, axis=-1, keepdims=True) + EPS)
    v_scr[...] = (zv * r * vg_ref[...]).astype(BF16)
    y = jnp.zeros_like(x)
    for cb in range(SGU_DIM // SGU_COL_BLOCK):
        cols = slice(cb * SGU_COL_BLOCK, (cb + 1) * SGU_COL_BLOCK)
        u = _gelu(jnp.dot(h, win_ref[:, cols], preferred_element_type=F32) + bin_ref[:, cols])
        for gi in range(SGU_COL_BLOCK // SGU_GROUP_DIM):
            grp = cb * (SGU_COL_BLOCK // SGU_GROUP_DIM) + gi
            w_s = ws_ref[grp]
            b_s = bs_ref[grp]
            for c in range(tm // CHUNK):
                rows = slice(c * CHUNK, (c + 1) * CHUNK)
                gcols = slice(grp * SGU_GROUP_DIM, (grp + 1) * SGU_GROUP_DIM)
                sv = jnp.dot(w_s, v_scr[rows, gcols], preferred_element_type=F32) + b_s
                lcols = slice(gi * SGU_GROUP_DIM, (gi + 1) * SGU_GROUP_DIM)
                t_scr[rows, lcols] = (u[rows, lcols] * sv).astype(BF16)
        y = y + jnp.dot(t_scr[...], wout_ref[cols, :], preferred_element_type=F32)
    out_ref[...] = x + gate_ref[...] * y


def _sgu(x2d, tm, mod, mod_tiles_per_row, norm_g, w_in, b_in, v_g, w_s, b_s, w_out):
    t = x2d.shape[0]
    const = lambda i: (0, 0)
    const3 = lambda i: (0, 0, 0)
    tile = pl.BlockSpec((tm, D_MODEL), lambda i: (i, 0))
    return pl.pallas_call(
        _sgu_kernel,
        grid=(t // tm,),
        in_specs=[
            tile,
            pl.BlockSpec((1, D_MODEL), const),
            _mod_spec(0, mod_tiles_per_row),
            _mod_spec(1, mod_tiles_per_row),
            _mod_spec(2, mod_tiles_per_row),
            _resident((D_MODEL, 2 * SGU_DIM), const),
            pl.BlockSpec((1, 2 * SGU_DIM), const),
            pl.BlockSpec((1, SGU_DIM), const),
            pl.BlockSpec((SGU_GROUPS, CHUNK, CHUNK), const3),
            pl.BlockSpec((SGU_GROUPS, CHUNK, 1), const3),
            _resident((SGU_DIM, D_MODEL), const),
        ],
        out_specs=tile,
        out_shape=jax.ShapeDtypeStruct(x2d.shape, F32),
        scratch_shapes=[pltpu.VMEM((tm, SGU_DIM), BF16), pltpu.VMEM((tm, SGU_COL_BLOCK), BF16)],
        compiler_params=_params(1),
        name="sgu",
    )(x2d, norm_g, mod, mod, mod, w_in, b_in, v_g, w_s, b_s, w_out)


def _deinterleave_heads(n_heads):
    half = np.concatenate([np.arange(0, HEAD_DIM, 2), np.arange(1, HEAD_DIM, 2)])
    return (np.arange(n_heads)[:, None] * HEAD_DIM + half[None, :]).reshape(-1)


def kernel(x, c, ctx, c_ctx, ada_w, ada_b, mix_norm_g, mlp_norm_g, mlp_w1, mlp_w2, attn_wqkv,
           attn_q_g, attn_k_g, attn_wo, sgu_w_in, sgu_b_in, sgu_v_g, sgu_w_s, sgu_b_s, sgu_w_out,
           final_g):
    b, n, d = x.shape
    lc = ctx.shape[1]
    tm_x, tm_c = 512, 256
    tq, tk = 256, 768

    cond = jnp.concatenate([c, c_ctx[None, :], jnp.zeros((MOD_ROWS - b - 1, d), F32)], axis=0)
    mods = _adaln(cond, ada_w, ada_b)
    cos, sin = _rope_tables(n)
    cos_c = jnp.ones((lc, HEAD_DIM), F32)
    sin_c = jnp.zeros((lc, HEAD_DIM), F32)
    perm = np.concatenate([_deinterleave_heads(N_HEADS),
                           Q_DIM + _deinterleave_heads(N_KV_HEADS),
                           np.arange(Q_DIM + KV_DIM, QKV_DIM)])
    head_perm = _deinterleave_heads(1)

    x2 = x.reshape(b * n, d)
    c2 = ctx.reshape(b * lc, d)
    row = lambda v: v.reshape(1, -1)
    for i in range(DEPTH):
        last = i == DEPTH - 1
        use_attn = (i % N_MIXERS) == 0
        j = i // N_MIXERS
        mod_x = mods[i, :b].reshape(b, 1, N_MOD * d)
        mod_c = mods[i, b:b + 1].reshape(1, 1, N_MOD * d)
        x_tpr, c_tpr = n // tm_x, (b * lc) // tm_c
        mix_g = row(mix_norm_g[i])
        mlp_g = row(mlp_norm_g[i])
        w1 = mlp_w1[i].astype(BF16)
        w2 = mlp_w2[i].astype(BF16)
        fg = row(final_g) if last else None
        if use_attn:
            wqkv = attn_wqkv[j][:, perm].astype(BF16)
            q_g = row(attn_q_g[j][head_perm])
            k_g = row(attn_k_g[j][head_perm])
            wo = attn_wo[j].astype(BF16)
            q, k, vt = _qkv(x2, n, tm_x, mod_x, x_tpr, mix_g, wqkv, q_g, k_g, cos, sin)
            qc, kc, vtc = _qkv(c2, lc, tm_c, mod_c, c_tpr, mix_g, wqkv, q_g, k_g, cos_c, sin_c)
            k_all = jnp.concatenate([kc, k], axis=2)
            vt_all = jnp.concatenate([vtc, vt], axis=3)
            o = _flash(q, k_all, vt_all, n, tq, tk)
            x2 = _mlp(x2, tm_x, mod_x, x_tpr, mlp_g, w1, w2, attn_o=o, wo=wo, final_g=fg)
            if not last:
                oc = _flash(qc, kc, vtc, lc, lc, lc)
                c2 = _mlp(c2, tm_c, mod_c, c_tpr, mlp_g, w1, w2, attn_o=oc, wo=wo)
        else:
            sgu_args = (sgu_w_in[j].astype(BF16), row(sgu_b_in[j]), row(sgu_v_g[j]),
                        sgu_w_s[j].astype(BF16), sgu_b_s[j][:, :, None],
                        sgu_w_out[j].astype(BF16))
            x2 = _sgu(x2, 2 * CHUNK, mod_x, n // (2 * CHUNK), mix_g, *sgu_args)
            x2 = _mlp(x2, tm_x, mod_x, x_tpr, mlp_g, w1, w2, final_g=fg)
            if not last:
                c2 = _sgu(c2, tm_c, mod_c, c_tpr, mix_g, *sgu_args)
                c2 = _mlp(c2, tm_c, mod_c, c_tpr, mlp_g, w1, w2)
    return x2.reshape(b, n, d)
```

```python
import functools

import jax
import jax.numpy as jnp
import numpy as np
from jax import lax
from jax.experimental import pallas as pl
from jax.experimental.pallas import tpu as pltpu

D_MODEL = 1024
DEPTH = 2
GRID_W = 64
N_MIXERS = 2
N_HEADS = 8
N_KV_HEADS = 2
HEAD_DIM = 128
GROUP = N_HEADS // N_KV_HEADS
Q_DIM = N_HEADS * HEAD_DIM
KV_DIM = N_KV_HEADS * HEAD_DIM
QKV_DIM = Q_DIM + 2 * KV_DIM
ROPE_AXIS_DIM = HEAD_DIM // 2
ROPE_THETA = 10000.0
CHUNK = 128
SGU_DIM = 3 * D_MODEL
SGU_GROUPS = 8
SGU_GROUP_DIM = SGU_DIM // SGU_GROUPS
D_FF = 4 * D_MODEL
N_MOD = 6
EPS = 1e-6

F32 = jnp.float32
BF16 = jnp.bfloat16

VMEM_LIMIT_BYTES = 56 * 1024 * 1024
MOD_ROWS = 8

SOFTMAX_Q_SCALE = float(HEAD_DIM ** -0.5 * np.log2(np.e))
NEG_BIG = -1e30
BF16_SUBLANE_TILE = 16
VT_ROWS = HEAD_DIM + BF16_SUBLANE_TILE


def _params(n_axes):
    return pltpu.CompilerParams(dimension_semantics=("parallel",) * n_axes,
                                vmem_limit_bytes=VMEM_LIMIT_BYTES)


def _resident(shape, index_map):
    return pl.BlockSpec(shape, index_map, pipeline_mode=pl.Buffered(1))


def _rms_mod(x, g, shift, scale):
    y = x * lax.rsqrt(jnp.mean(x * x, axis=-1, keepdims=True) + EPS) * g
    return y * (1.0 + scale) + shift


def _mod_spec(k, tiles_per_row):
    return pl.BlockSpec((None, 1, D_MODEL), lambda i: (i // tiles_per_row, 0, k))


def _adaln_kernel(c_ref, w_ref, b_ref, o_ref):
    c = c_ref[...]
    s = c * jax.nn.sigmoid(c)
    o_ref[...] = jnp.dot(s.astype(BF16), w_ref[...].astype(BF16),
                         preferred_element_type=F32) + b_ref[...]


def _adaln(cond, ada_w, ada_b):
    depth = ada_w.shape[0]
    return pl.pallas_call(
        _adaln_kernel,
        grid=(depth, N_MOD),
        in_specs=[
            pl.BlockSpec((MOD_ROWS, D_MODEL), lambda i, j: (0, 0)),
            pl.BlockSpec((None, D_MODEL, D_MODEL), lambda i, j: (i, 0, j)),
            pl.BlockSpec((None, 1, D_MODEL), lambda i, j: (i, 0, j)),
        ],
        out_specs=pl.BlockSpec((None, MOD_ROWS, D_MODEL), lambda i, j: (i, 0, j)),
        out_shape=jax.ShapeDtypeStruct((depth, MOD_ROWS, N_MOD * D_MODEL), F32),
        compiler_params=_params(2),
        name="adaln",
    )(cond, ada_w, ada_b.reshape(depth, 1, N_MOD * D_MODEL))


def _cos_sin_kernel(ang_ref, cos_ref, sin_ref):
    a = ang_ref[...]
    cos_ref[...] = jnp.cos(a)
    sin_ref[...] = jnp.sin(a)


def _rope_tables(n):
    rows_count = n // GRID_W
    freqs = 1.0 / (ROPE_THETA ** (jnp.arange(0, ROPE_AXIS_DIM, 2, dtype=F32) / ROPE_AXIS_DIM))
    pos = jnp.arange(rows_count + GRID_W, dtype=jnp.int32)
    pos = jnp.where(pos < rows_count, pos, pos - rows_count).astype(F32)
    ang = pos[:, None] * freqs[None, :]
    cos_s, sin_s = pl.pallas_call(
        _cos_sin_kernel,
        out_shape=(jax.ShapeDtypeStruct(ang.shape, F32),) * 2,
        name="rope_cos_sin",
    )(ang)

    def expand(t):
        by_row = jnp.repeat(t[:rows_count], GRID_W, axis=0)
        by_col = jnp.tile(t[rows_count:], (rows_count, 1))
        return jnp.concatenate([by_row, by_col], axis=-1)

    c, s = expand(cos_s), expand(sin_s)
    return jnp.concatenate([c, c], axis=-1), jnp.concatenate([-s, s], axis=-1)


def _head_norm_rope(t, g, cos, sin):
    t = t * lax.rsqrt(jnp.mean(t * t, axis=-1, keepdims=True) + EPS) * g
    return t * cos + pltpu.roll(t, HEAD_DIM // 2, axis=1) * sin


def _qkv_kernel(x_ref, g_ref, shift_ref, scale_ref, w_ref, qg_ref, kg_ref, cos_ref, sin_ref,
                q_ref, k_ref, vt_ref):
    h = _rms_mod(x_ref[...], g_ref[...], shift_ref[...], scale_ref[...]).astype(BF16)
    cos = cos_ref[...]
    sin = sin_ref[...]
    q = jnp.dot(h, w_ref[:, :Q_DIM], preferred_element_type=F32)
    qg = qg_ref[...]
    for hd in range(N_HEADS):
        sl = slice(hd * HEAD_DIM, (hd + 1) * HEAD_DIM)
        qh = _head_norm_rope(q[:, sl], qg, cos, sin)
        q_ref[:, sl] = (qh * SOFTMAX_Q_SCALE).astype(BF16)
    kv = jnp.dot(h, w_ref[:, Q_DIM:], preferred_element_type=F32)
    kg = kg_ref[...]
    for hd in range(N_KV_HEADS):
        sl = slice(hd * HEAD_DIM, (hd + 1) * HEAD_DIM)
        k_ref[hd] = _head_norm_rope(kv[:, sl], kg, cos, sin).astype(BF16)
        v = kv[:, KV_DIM + hd * HEAD_DIM:KV_DIM + (hd + 1) * HEAD_DIM]
        vt_ref[hd, :HEAD_DIM, :] = v.T.astype(BF16)
        vt_ref[hd, HEAD_DIM:, :] = jnp.ones((VT_ROWS - HEAD_DIM, v.shape[0]), BF16)


def _qkv(x2d, seq, tm, mod, mod_tiles_per_row, norm_g, wqkv, q_g, k_g, cos, sin):
    t = x2d.shape[0]
    tpb = seq // tm
    const = lambda i: (0, 0)
    return pl.pallas_call(
        _qkv_kernel,
        grid=(t // tm,),
        in_specs=[
            pl.BlockSpec((tm, D_MODEL), lambda i: (i, 0)),
            pl.BlockSpec((1, D_MODEL), const),
            _mod_spec(0, mod_tiles_per_row),
            _mod_spec(1, mod_tiles_per_row),
            _resident((D_MODEL, QKV_DIM), const),
            pl.BlockSpec((1, HEAD_DIM), const),
            pl.BlockSpec((1, HEAD_DIM), const),
            pl.BlockSpec((tm, HEAD_DIM), lambda i: (i % tpb, 0)),
            pl.BlockSpec((tm, HEAD_DIM), lambda i: (i % tpb, 0)),
        ],
        out_specs=[
            pl.BlockSpec((tm, Q_DIM), lambda i: (i, 0)),
            pl.BlockSpec((None, N_KV_HEADS, tm, HEAD_DIM), lambda i: (i // tpb, 0, i % tpb, 0)),
            pl.BlockSpec((None, N_KV_HEADS, VT_ROWS, tm), lambda i: (i // tpb, 0, 0, i % tpb)),
        ],
        out_shape=[
            jax.ShapeDtypeStruct((t, Q_DIM), BF16),
            jax.ShapeDtypeStruct((t // seq, N_KV_HEADS, seq, HEAD_DIM), BF16),
            jax.ShapeDtypeStruct((t // seq, N_KV_HEADS, VT_ROWS, seq), BF16),
        ],
        compiler_params=_params(1),
        name="qkv_proj",
    )(x2d, norm_g, mod, mod, wqkv, q_g, k_g, cos, sin)


def _flash_kernel(q_ref, k_ref, vt_ref, o_ref, s_buf, tmax_scr, m_scr, acc_scr, *, tk, nk):
    m_scr[...] = jnp.full(m_scr.shape, NEG_BIG, F32)
    acc_scr[...] = jnp.zeros(acc_scr.shape, F32)

    def scores(g, t, slot):
        ks = pl.multiple_of(t * tk, tk)
        s = lax.dot_general(k_ref[pl.ds(ks, tk), :], q_ref[:, g * HEAD_DIM:(g + 1) * HEAD_DIM],
                            (((1,), (1,)), ((), ())), preferred_element_type=F32)
        s_buf[slot] = s
        tmax_scr[slot] = jnp.max(s, axis=0, keepdims=True)

    def accumulate(g, t, slot):
        ks = pl.multiple_of(t * tk, tk)
        m_old = m_scr[g]
        m_new = jnp.maximum(m_old, tmax_scr[slot])
        alpha = jnp.exp2(m_old - m_new)
        m_scr[g] = m_new
        p = jnp.exp2(s_buf[slot] - m_new).astype(BF16)
        pv = jnp.dot(vt_ref[:, pl.ds(ks, tk)], p, preferred_element_type=F32)
        acc_scr[g] = alpha * acc_scr[g] + pv

    def step(g, t, t_parity):
        slot = (g * nk + t_parity) % 2
        scores(g, t + 1, 1 - slot)
        accumulate(g, t, slot)

    scores(0, 0, 0)
    for g in range(GROUP):
        n_pairs, odd = divmod(nk - 1, 2)

        def body(t2, carry, g=g):
            step(g, 2 * t2, 0)
            step(g, 2 * t2 + 1, 1)
            return carry

        lax.fori_loop(0, n_pairs, body, 0)
        if odd:
            step(g, nk - 2, (nk - 2) % 2)
        last_slot = (g * nk + nk - 1) % 2
        if g + 1 < GROUP:
            scores(g + 1, 0, 1 - last_slot)
        accumulate(g, nk - 1, last_slot)
        acc = acc_scr[g]
        o = acc[:HEAD_DIM] * (1.0 / acc[HEAD_DIM:HEAD_DIM + 1])
        o_ref[:, g * HEAD_DIM:(g + 1) * HEAD_DIM] = o.T.astype(BF16)


def _flash(q, k, vt, seq_q, tq, tk):
    b, _, lk, _ = k.shape
    nq = seq_q // tq
    gw = GROUP * HEAD_DIM
    return pl.pallas_call(
        functools.partial(_flash_kernel, tk=tk, nk=lk // tk),
        grid=(b, N_KV_HEADS, nq),
        in_specs=[
            pl.BlockSpec((tq, gw), lambda bi, h, i: (bi * nq + i, h)),
            pl.BlockSpec((None, None, lk, HEAD_DIM), lambda bi, h, i: (bi, h, 0, 0)),
            pl.BlockSpec((None, None, VT_ROWS, lk), lambda bi, h, i: (bi, h, 0, 0)),
        ],
        out_specs=pl.BlockSpec((tq, gw), lambda bi, h, i: (bi * nq + i, h)),
        out_shape=jax.ShapeDtypeStruct(q.shape, BF16),
        scratch_shapes=[pltpu.VMEM((2, tk, tq), F32), pltpu.VMEM((2, 1, tq), F32),
                        pltpu.VMEM((GROUP, 1, tq), F32), pltpu.VMEM((GROUP, VT_ROWS, tq), F32)],
        compiler_params=_params(3),
        name="flash_attn",
    )(q, k, vt)


def _mlp_kernel(*refs, with_proj, with_final_norm, ff_chunk):
    refs = list(refs)
    x_ref = refs.pop(0)
    if with_proj:
        o_ref, wo_ref, gate_mix_ref = refs.pop(0), refs.pop(0), refs.pop(0)
    g_ref, shift_ref, scale_ref, gate_ref, w1_ref, w2_ref = refs[:6]
    refs = refs[6:]
    if with_final_norm:
        fg_ref = refs.pop(0)
    (out_ref,) = refs

    x = x_ref[...]
    if with_proj:
        x = x + gate_mix_ref[...] * jnp.dot(o_ref[...], wo_ref[...], preferred_element_type=F32)
    h = _rms_mod(x, g_ref[...], shift_ref[...], scale_ref[...]).astype(BF16)
    y = jnp.zeros_like(x)
    for c in range(D_FF // ff_chunk):
        sl = slice(c * ff_chunk, (c + 1) * ff_chunk)
        a = jnp.maximum(jnp.dot(h, w1_ref[:, sl], preferred_element_type=F32), 0.0)
        y = y + jnp.dot((a * a).astype(BF16), w2_ref[sl, :], preferred_element_type=F32)
    x = x + gate_ref[...] * y
    if with_final_norm:
        x = x * lax.rsqrt(jnp.mean(x * x, axis=-1, keepdims=True) + EPS) * fg_ref[...]
    out_ref[...] = x


def _mlp(x2d, tm, mod, mod_tiles_per_row, norm_g, w1, w2, *, attn_o=None, wo=None, final_g=None):
    t = x2d.shape[0]
    const = lambda i: (0, 0)
    tile = pl.BlockSpec((tm, D_MODEL), lambda i: (i, 0))
    args, specs = [x2d], [tile]
    if attn_o is not None:
        args += [attn_o, wo, mod]
        specs += [pl.BlockSpec((tm, Q_DIM), lambda i: (i, 0)), _resident((Q_DIM, D_MODEL), const),
                  _mod_spec(2, mod_tiles_per_row)]
    args += [norm_g, mod, mod, mod, w1, w2]
    specs += [pl.BlockSpec((1, D_MODEL), const), _mod_spec(3, mod_tiles_per_row),
              _mod_spec(4, mod_tiles_per_row), _mod_spec(5, mod_tiles_per_row),
              _resident((D_MODEL, D_FF), const), _resident((D_FF, D_MODEL), const)]
    if final_g is not None:
        args.append(final_g)
        specs.append(pl.BlockSpec((1, D_MODEL), const))
    return pl.pallas_call(
        functools.partial(_mlp_kernel, with_proj=attn_o is not None,
                          with_final_norm=final_g is not None, ff_chunk=1024),
        grid=(t // tm,),
        in_specs=specs,
        out_specs=tile,
        out_shape=jax.ShapeDtypeStruct(x2d.shape, F32),
        compiler_params=_params(1),
        name="mlp",
    )(*args)


def _gelu(z):
    return 0.5 * z * (1.0 + lax.erf(z * float(np.sqrt(0.5))))


SGU_COL_BLOCK = 2 * SGU_GROUP_DIM


def _sgu_kernel(x_ref, g_ref, shift_ref, scale_ref, gate_ref, win_ref, bin_ref, vg_ref, ws_ref,
                bs_ref, wout_ref, out_ref, v_scr, t_scr):
    tm = x_ref.shape[0]
    x = x_ref[...]
    h = _rms_mod(x, g_ref[...], shift_ref[...], scale_ref[...]).astype(BF16)
    zv = _gelu(jnp.dot(h, win_ref[:, SGU_DIM:], preferred_element_type=F32) + bin_ref[:, SGU_DIM:])
    r = lax.rsqrt(jnp.mean(zv * zv, axis=-1, keepdims=True) + EPS)
    v_scr[...] = (zv * r * vg_ref[...]).astype(BF16)
    y = jnp.zeros_like(x)
    for cb in range(SGU_DIM // SGU_COL_BLOCK):
        cols = slice(cb * SGU_COL_BLOCK, (cb + 1) * SGU_COL_BLOCK)
        u = _gelu(jnp.dot(h, win_ref[:, cols], preferred_element_type=F32) + bin_ref[:, cols])
        for gi in range(SGU_COL_BLOCK // SGU_GROUP_DIM):
            grp = cb * (SGU_COL_BLOCK // SGU_GROUP_DIM) + gi
            w_s = ws_ref[grp]
            b_s = bs_ref[grp]
            for c in range(tm // CHUNK):
                rows = slice(c * CHUNK, (c + 1) * CHUNK)
                gcols = slice(grp * SGU_GROUP_DIM, (grp + 1) * SGU_GROUP_DIM)
                sv = jnp.dot(w_s, v_scr[rows, gcols], preferred_element_type=F32) + b_s
                lcols = slice(gi * SGU_GROUP_DIM, (gi + 1) * SGU_GROUP_DIM)
                t_scr[rows, lcols] = (u[rows, lcols] * sv).astype(BF16)
        y = y + jnp.dot(t_scr[...], wout_ref[cols, :], preferred_element_type=F32)
    out_ref[...] = x + gate_ref[...] * y


def _sgu(x2d, tm, mod, mod_tiles_per_row, norm_g, w_in, b_in, v_g, w_s, b_s, w_out):
    t = x2d.shape[0]
    const = lambda i: (0, 0)
    const3 = lambda i: (0, 0, 0)
    tile = pl.BlockSpec((tm, D_MODEL), lambda i: (i, 0))
    return pl.pallas_call(
        _sgu_kernel,
        grid=(t // tm,),
        in_specs=[
            tile,
            pl.BlockSpec((1, D_MODEL), const),
            _mod_spec(0, mod_tiles_per_row),
            _mod_spec(1, mod_tiles_per_row),
            _mod_spec(2, mod_tiles_per_row),
            _resident((D_MODEL, 2 * SGU_DIM), const),
            pl.BlockSpec((1, 2 * SGU_DIM), const),
            pl.BlockSpec((1, SGU_DIM), const),
            pl.BlockSpec((SGU_GROUPS, CHUNK, CHUNK), const3),
            pl.BlockSpec((SGU_GROUPS, CHUNK, 1), const3),
            _resident((SGU_DIM, D_MODEL), const),
        ],
        out_specs=tile,
        out_shape=jax.ShapeDtypeStruct(x2d.shape, F32),
        scratch_shapes=[pltpu.VMEM((tm, SGU_DIM), BF16), pltpu.VMEM((tm, SGU_COL_BLOCK), BF16)],
        compiler_params=_params(1),
        name="sgu",
    )(x2d, norm_g, mod, mod, mod, w_in, b_in, v_g, w_s, b_s, w_out)


def _deinterleave_heads(n_heads):
    half = np.concatenate([np.arange(0, HEAD_DIM, 2), np.arange(1, HEAD_DIM, 2)])
    return (np.arange(n_heads)[:, None] * HEAD_DIM + half[None, :]).reshape(-1)


def kernel(x, c, ctx, c_ctx, ada_w, ada_b, mix_norm_g, mlp_norm_g, mlp_w1, mlp_w2, attn_wqkv,
           attn_q_g, attn_k_g, attn_wo, sgu_w_in, sgu_b_in, sgu_v_g, sgu_w_s, sgu_b_s, sgu_w_out,
           final_g):
    b, n, d = x.shape
    lc = ctx.shape[1]
    tm_x, tm_c = 512, 256
    tq, tk = 1024, 768

    cond = jnp.concatenate([c, c_ctx[None, :], jnp.zeros((MOD_ROWS - b - 1, d), F32)], axis=0)
    mods = _adaln(cond, ada_w, ada_b)
    cos, sin = _rope_tables(n)
    cos_c = jnp.ones((lc, HEAD_DIM), F32)
    sin_c = jnp.zeros((lc, HEAD_DIM), F32)
    perm = np.concatenate([_deinterleave_heads(N_HEADS),
                           Q_DIM + _deinterleave_heads(N_KV_HEADS),
                           np.arange(Q_DIM + KV_DIM, QKV_DIM)])
    head_perm = _deinterleave_heads(1)

    x2 = x.reshape(b * n, d)
    c2 = ctx.reshape(b * lc, d)
    row = lambda v: v.reshape(1, -1)
    for i in range(DEPTH):
        last = i == DEPTH - 1
        use_attn = (i % N_MIXERS) == 0
        j = i // N_MIXERS
        mod_x = mods[i, :b].reshape(b, 1, N_MOD * d)
        mod_c = mods[i, b:b + 1].reshape(1, 1, N_MOD * d)
        x_tpr, c_tpr = n // tm_x, (b * lc) // tm_c
        mix_g = row(mix_norm_g[i])
        mlp_g = row(mlp_norm_g[i])
        w1 = mlp_w1[i].astype(BF16)
        w2 = mlp_w2[i].astype(BF16)
        fg = row(final_g) if last else None
        if use_attn:
            wqkv = attn_wqkv[j][:, perm].astype(BF16)
            q_g = row(attn_q_g[j][head_perm])
            k_g = row(attn_k_g[j][head_perm])
            wo = attn_wo[j].astype(BF16)
            q, k, vt = _qkv(x2, n, tm_x, mod_x, x_tpr, mix_g, wqkv, q_g, k_g, cos, sin)
            qc, kc, vtc = _qkv(c2, lc, tm_c, mod_c, c_tpr, mix_g, wqkv, q_g, k_g, cos_c, sin_c)
            k_all = jnp.concatenate([kc, k], axis=2)
            vt_all = jnp.concatenate([vtc, vt], axis=3)
            o = _flash(q, k_all, vt_all, n, tq, tk)
            x2 = _mlp(x2, tm_x, mod_x, x_tpr, mlp_g, w1, w2, attn_o=o, wo=wo, final_g=fg)
            if not last:
                oc = _flash(qc, kc, vtc, lc, lc, lc)
                c2 = _mlp(c2, tm_c, mod_c, c_tpr, mlp_g, w1, w2, attn_o=oc, wo=wo)
        else:
            sgu_args = (sgu_w_in[j].astype(BF16), row(sgu_b_in[j]), row(sgu_v_g[j]),
                        sgu_w_s[j].astype(BF16), sgu_b_s[j][:, :, None],
                        sgu_w_out[j].astype(BF16))
            x2 = _sgu(x2, 2 * CHUNK, mod_x, n // (2 * CHUNK), mix_g, *sgu_args)
            x2 = _mlp(x2, tm_x, mod_x, x_tpr, mlp_g, w1, w2, final_g=fg)
            if not last:
                c2 = _sgu(c2, tm_c, mod_c, c_tpr, mix_g, *sgu_args)
                c2 = _mlp(c2, tm_c, mod_c, c_tpr, mlp_g, w1, w2)
    return x2.reshape(b, n, d)
```

```python
import functools

import jax
import jax.numpy as jnp
import numpy as np
from jax import lax
from jax.experimental import pallas as pl
from jax.experimental.pallas import tpu as pltpu

D_MODEL = 1024
DEPTH = 2
GRID_W = 64
N_MIXERS = 2
N_HEADS = 8
N_KV_HEADS = 2
HEAD_DIM = 128
GROUP = N_HEADS // N_KV_HEADS
Q_DIM = N_HEADS * HEAD_DIM
KV_DIM = N_KV_HEADS * HEAD_DIM
QKV_DIM = Q_DIM + 2 * KV_DIM
ROPE_AXIS_DIM = HEAD_DIM // 2
ROPE_THETA = 10000.0
CHUNK = 128
SGU_DIM = 3 * D_MODEL
SGU_GROUPS = 8
SGU_GROUP_DIM = SGU_DIM // SGU_GROUPS
D_FF = 4 * D_MODEL
N_MOD = 6
EPS = 1e-6

F32 = jnp.float32
BF16 = jnp.bfloat16

VMEM_LIMIT_BYTES = 56 * 1024 * 1024
MOD_ROWS = 8

SOFTMAX_Q_SCALE = float(HEAD_DIM ** -0.5 * np.log2(np.e))
NEG_BIG = -1e30
BF16_SUBLANE_TILE = 16
VT_ROWS = HEAD_DIM + BF16_SUBLANE_TILE


def _params(n_axes):
    return pltpu.CompilerParams(dimension_semantics=("parallel",) * n_axes,
                                vmem_limit_bytes=VMEM_LIMIT_BYTES)


def _resident(shape, index_map):
    return pl.BlockSpec(shape, index_map, pipeline_mode=pl.Buffered(1))


def _rms_mod(x, g, shift, scale):
    y = x * lax.rsqrt(jnp.mean(x * x, axis=-1, keepdims=True) + EPS) * g
    return y * (1.0 + scale) + shift


def _mod_spec(k, tiles_per_row):
    return pl.BlockSpec((None, 1, D_MODEL), lambda i: (i // tiles_per_row, 0, k))


def _adaln_kernel(c_ref, w_ref, b_ref, o_ref):
    c = c_ref[...]
    s = c * jax.nn.sigmoid(c)
    o_ref[...] = jnp.dot(s.astype(BF16), w_ref[...].astype(BF16),
                         preferred_element_type=F32) + b_ref[...]


def _adaln(cond, ada_w, ada_b):
    depth = ada_w.shape[0]
    return pl.pallas_call(
        _adaln_kernel,
        grid=(depth, N_MOD),
        in_specs=[
            pl.BlockSpec((MOD_ROWS, D_MODEL), lambda i, j: (0, 0)),
            pl.BlockSpec((None, D_MODEL, D_MODEL), lambda i, j: (i, 0, j)),
            pl.BlockSpec((None, 1, D_MODEL), lambda i, j: (i, 0, j)),
        ],
        out_specs=pl.BlockSpec((None, MOD_ROWS, D_MODEL), lambda i, j: (i, 0, j)),
        out_shape=jax.ShapeDtypeStruct((depth, MOD_ROWS, N_MOD * D_MODEL), F32),
        compiler_params=_params(2),
        name="adaln",
    )(cond, ada_w, ada_b.reshape(depth, 1, N_MOD * D_MODEL))


def _cos_sin_kernel(ang_ref, cos_ref, sin_ref):
    a = ang_ref[...]
    cos_ref[...] = jnp.cos(a)
    sin_ref[...] = jnp.sin(a)


def _rope_tables(n):
    rows_count = n // GRID_W
    freqs = 1.0 / (ROPE_THETA ** (jnp.arange(0, ROPE_AXIS_DIM, 2, dtype=F32) / ROPE_AXIS_DIM))
    pos = jnp.arange(rows_count + GRID_W, dtype=jnp.int32)
    pos = jnp.where(pos < rows_count, pos, pos - rows_count).astype(F32)
    ang = pos[:, None] * freqs[None, :]
    cos_s, sin_s = pl.pallas_call(
        _cos_sin_kernel,
        out_shape=(jax.ShapeDtypeStruct(ang.shape, F32),) * 2,
        name="rope_cos_sin",
    )(ang)

    def expand(t):
        by_row = jnp.repeat(t[:rows_count], GRID_W, axis=0)
        by_col = jnp.tile(t[rows_count:], (rows_count, 1))
        return jnp.concatenate([by_row, by_col], axis=-1)

    return expand(cos_s).T, expand(sin_s).T


Q_HEADS_PER_DOT = 4


def _qkv_kernel(x_ref, g_ref, shift_ref, scale_ref, wt_ref, qg_ref, kg_ref, cos_ref, sin_ref,
                *rest):
    qt_ref, k_ref, vt_ref = rest[-3:]
    h = _rms_mod(x_ref[...], g_ref[...], shift_ref[...], scale_ref[...]).astype(BF16)
    cos = cos_ref[...]
    sin = sin_ref[...]
    half = HEAD_DIM // 2
    nt_dims = (((1,), (1,)), ((), ()))

    def norm_rope(t, g):
        t = t * lax.rsqrt(jnp.mean(t * t, axis=0, keepdims=True) + EPS) * g
        x1, x2 = t[:half], t[half:]
        return x1 * cos - x2 * sin, x1 * sin + x2 * cos

    qg = qg_ref[...]
    for blk in range(N_HEADS // Q_HEADS_PER_DOT):
        rows = slice(blk * Q_HEADS_PER_DOT * HEAD_DIM, (blk + 1) * Q_HEADS_PER_DOT * HEAD_DIM)
        yt = lax.dot_general(wt_ref[rows, :], h, nt_dims, preferred_element_type=F32)
        for j in range(Q_HEADS_PER_DOT):
            r0 = (blk * Q_HEADS_PER_DOT + j) * HEAD_DIM
            o1, o2 = norm_rope(yt[j * HEAD_DIM:(j + 1) * HEAD_DIM], qg)
            qt_ref[r0:r0 + half, :] = (o1 * SOFTMAX_Q_SCALE).astype(BF16)
            qt_ref[r0 + half:r0 + HEAD_DIM, :] = (o2 * SOFTMAX_Q_SCALE).astype(BF16)
    yt = lax.dot_general(wt_ref[Q_DIM:, :], h, nt_dims, preferred_element_type=F32)
    kg = kg_ref[...]
    for hd in range(N_KV_HEADS):
        o1, o2 = norm_rope(yt[hd * HEAD_DIM:(hd + 1) * HEAD_DIM], kg)
        k_ref[hd] = jnp.concatenate([o1, o2], axis=0).T.astype(BF16)
        v_t = yt[KV_DIM + hd * HEAD_DIM:KV_DIM + (hd + 1) * HEAD_DIM]
        vt_ref[hd, :HEAD_DIM, :] = v_t.astype(BF16)
        vt_ref[hd, HEAD_DIM:, :] = jnp.ones((VT_ROWS - HEAD_DIM, v_t.shape[1]), BF16)


def _qkv(x2d, seq, tm, mod, mod_tiles_per_row, norm_g, wqkv_t, q_g, k_g, cos_t, sin_t, *,
         kv_len, kv_offset, kv_buffers=None):
    t = x2d.shape[0]
    tpb = seq // tm
    off = kv_offset // tm
    const = lambda i: (0, 0)
    in_specs = [
        pl.BlockSpec((tm, D_MODEL), lambda i: (i, 0)),
        pl.BlockSpec((1, D_MODEL), const),
        _mod_spec(0, mod_tiles_per_row),
        _mod_spec(1, mod_tiles_per_row),
        _resident((QKV_DIM, D_MODEL), const),
        pl.BlockSpec((HEAD_DIM, 1), const),
        pl.BlockSpec((HEAD_DIM, 1), const),
        pl.BlockSpec((HEAD_DIM // 2, tm), lambda i: (0, i % tpb)),
        pl.BlockSpec((HEAD_DIM // 2, tm), lambda i: (0, i % tpb)),
    ]
    args = [x2d, norm_g, mod, mod, wqkv_t, q_g, k_g, cos_t, sin_t]
    aliases = {}
    if kv_buffers is not None:
        aliases = {len(args): 1, len(args) + 1: 2}
        args += list(kv_buffers)
        in_specs += [pl.BlockSpec(memory_space=pl.ANY)] * 2
    return pl.pallas_call(
        _qkv_kernel,
        grid=(t // tm,),
        in_specs=in_specs,
        out_specs=[
            pl.BlockSpec((Q_DIM, tm), lambda i: (0, i)),
            pl.BlockSpec((None, N_KV_HEADS, tm, HEAD_DIM),
                         lambda i: (i // tpb, 0, off + i % tpb, 0)),
            pl.BlockSpec((None, N_KV_HEADS, VT_ROWS, tm),
                         lambda i: (i // tpb, 0, 0, off + i % tpb)),
        ],
        out_shape=[
            jax.ShapeDtypeStruct((Q_DIM, t), BF16),
            jax.ShapeDtypeStruct((t // seq, N_KV_HEADS, kv_len, HEAD_DIM), BF16),
            jax.ShapeDtypeStruct((t // seq, N_KV_HEADS, VT_ROWS, kv_len), BF16),
        ],
        input_output_aliases=aliases,
        compiler_params=_params(1),
        name="qkv_proj",
    )(*args)


STEPS_PER_LOOP_BODY = 4


def _flash_kernel(q_ref, k_ref, vt_ref, o_ref, s_buf, tmax_scr, m_scr, acc_scr, *, tk, nk):
    m_scr[...] = jnp.full(m_scr.shape, NEG_BIG, F32)
    acc_scr[...] = jnp.zeros(acc_scr.shape, F32)

    n_steps = GROUP * nk

    def head_tile(u):
        if isinstance(u, int):
            return divmod(u, nk)
        g = u // nk
        return g, u - g * nk

    def scores(u, slot):
        g, t = head_tile(u)
        ks = pl.multiple_of(t * tk, tk)
        q_t = q_ref[pl.ds(pl.multiple_of(g * HEAD_DIM, HEAD_DIM), HEAD_DIM), :]
        s = jnp.dot(k_ref[pl.ds(ks, tk), :], q_t, preferred_element_type=F32)
        s_buf[slot] = s
        tmax_scr[slot] = jnp.max(s, axis=0, keepdims=True)

    def accumulate(u, slot):
        g, t = head_tile(u)
        ks = pl.multiple_of(t * tk, tk)
        m_old = m_scr[g]
        m_new = jnp.maximum(m_old, tmax_scr[slot])
        alpha = jnp.exp2(m_old - m_new)
        m_scr[g] = m_new
        p = jnp.exp2(s_buf[slot] - m_new).astype(BF16)
        pv = jnp.dot(vt_ref[:, pl.ds(ks, tk)], p, preferred_element_type=F32)
        acc_scr[g] = alpha * acc_scr[g] + pv

    def step(u, parity):
        scores(u + 1, 1 - parity)
        accumulate(u, parity)

    scores(0, 0)
    n_loops, n_tail = divmod(n_steps - 1, STEPS_PER_LOOP_BODY)

    def body(i, carry):
        for r in range(STEPS_PER_LOOP_BODY):
            step(STEPS_PER_LOOP_BODY * i + r, r % 2)
        return carry

    lax.fori_loop(0, n_loops, body, 0)
    for u in range(n_steps - 1 - n_tail, n_steps - 1):
        step(u, u % 2)
    accumulate(n_steps - 1, (n_steps - 1) % 2)
    for g in range(GROUP):
        acc = acc_scr[g]
        o = acc[:HEAD_DIM] * (1.0 / acc[HEAD_DIM:HEAD_DIM + 1])
        o_ref[:, g * HEAD_DIM:(g + 1) * HEAD_DIM] = o.T.astype(BF16)


def _flash(q_t, k, vt, seq_q, tq, tk):
    b, _, lk, _ = k.shape
    nq = seq_q // tq
    gw = GROUP * HEAD_DIM
    return pl.pallas_call(
        functools.partial(_flash_kernel, tk=tk, nk=lk // tk),
        grid=(b, N_KV_HEADS, nq),
        in_specs=[
            pl.BlockSpec((gw, tq), lambda bi, h, i: (h, bi * nq + i)),
            pl.BlockSpec((None, None, lk, HEAD_DIM), lambda bi, h, i: (bi, h, 0, 0)),
            pl.BlockSpec((None, None, VT_ROWS, lk), lambda bi, h, i: (bi, h, 0, 0)),
        ],
        out_specs=pl.BlockSpec((tq, gw), lambda bi, h, i: (bi * nq + i, h)),
        out_shape=jax.ShapeDtypeStruct((q_t.shape[1], q_t.shape[0]), BF16),
        scratch_shapes=[pltpu.VMEM((2, tk, tq), F32), pltpu.VMEM((2, 1, tq), F32),
                        pltpu.VMEM((GROUP, 1, tq), F32), pltpu.VMEM((GROUP, VT_ROWS, tq), F32)],
        compiler_params=_params(3),
        name="flash_attn",
    )(q_t, k, vt)


def _mlp_kernel(*refs, with_proj, with_final_norm, ff_chunk):
    refs = list(refs)
    x_ref = refs.pop(0)
    if with_proj:
        o_ref, wo_ref, gate_mix_ref = refs.pop(0), refs.pop(0), refs.pop(0)
    g_ref, shift_ref, scale_ref, gate_ref, w1_ref, w2_ref = refs[:6]
    refs = refs[6:]
    if with_final_norm:
        fg_ref = refs.pop(0)
    (out_ref,) = refs

    x = x_ref[...]
    if with_proj:
        x = x + gate_mix_ref[...] * jnp.dot(o_ref[...], wo_ref[...], preferred_element_type=F32)
    h = _rms_mod(x, g_ref[...], shift_ref[...], scale_ref[...]).astype(BF16)
    y = jnp.zeros_like(x)
    for c in range(D_FF // ff_chunk):
        sl = slice(c * ff_chunk, (c + 1) * ff_chunk)
        a = jnp.maximum(jnp.dot(h, w1_ref[:, sl], preferred_element_type=F32), 0.0)
        y = y + jnp.dot((a * a).astype(BF16), w2_ref[sl, :], preferred_element_type=F32)
    x = x + gate_ref[...] * y
    if with_final_norm:
        x = x * lax.rsqrt(jnp.mean(x * x, axis=-1, keepdims=True) + EPS) * fg_ref[...]
    out_ref[...] = x


def _mlp(x2d, tm, mod, mod_tiles_per_row, norm_g, w1, w2, *, attn_o=None, wo=None, final_g=None):
    t = x2d.shape[0]
    const = lambda i: (0, 0)
    tile = pl.BlockSpec((tm, D_MODEL), lambda i: (i, 0))
    args, specs = [x2d], [tile]
    if attn_o is not None:
        args += [attn_o, wo, mod]
        specs += [pl.BlockSpec((tm, Q_DIM), lambda i: (i, 0)), _resident((Q_DIM, D_MODEL), const),
                  _mod_spec(2, mod_tiles_per_row)]
    args += [norm_g, mod, mod, mod, w1, w2]
    specs += [pl.BlockSpec((1, D_MODEL), const), _mod_spec(3, mod_tiles_per_row),
              _mod_spec(4, mod_tiles_per_row), _mod_spec(5, mod_tiles_per_row),
              _resident((D_MODEL, D_FF), const), _resident((D_FF, D_MODEL), const)]
    if final_g is not None:
        args.append(final_g)
        specs.append(pl.BlockSpec((1, D_MODEL), const))
    return pl.pallas_call(
        functools.partial(_mlp_kernel, with_proj=attn_o is not None,
                          with_final_norm=final_g is not None, ff_chunk=1024),
        grid=(t // tm,),
        in_specs=specs,
        out_specs=tile,
        out_shape=jax.ShapeDtypeStruct(x2d.shape, F32),
        compiler_params=_params(1),
        name="mlp",
    )(*args)


def _gelu(z):
    return 0.5 * z * (1.0 + lax.erf(z * float(np.sqrt(0.5))))


SGU_COL_BLOCK = 2 * SGU_GROUP_DIM


def _sgu_kernel(x_ref, g_ref, shift_ref, scale_ref, gate_ref, win_ref, bin_ref, vg_ref, ws_ref,
                bs_ref, wout_ref, out_ref, v_scr, t_scr):
    tm = x_ref.shape[0]
    x = x_ref[...]
    h = _rms_mod(x, g_ref[...], shift_ref[...], scale_ref[...]).astype(BF16)
    zv = _gelu(jnp.dot(h, win_ref[:, SGU_DIM:], preferred_element_type=F32) + bin_ref[:, SGU_DIM:])
    r = lax.rsqrt(jnp.mean(zv * zv, axis=-1, keepdims=True) + EPS)
    v_scr[...] = (zv * r * vg_ref[...]).astype(BF16)
    y = jnp.zeros_like(x)
    for cb in range(SGU_DIM // SGU_COL_BLOCK):
        cols = slice(cb * SGU_COL_BLOCK, (cb + 1) * SGU_COL_BLOCK)
        u = _gelu(jnp.dot(h, win_ref[:, cols], preferred_element_type=F32) + bin_ref[:, cols])
        for gi in range(SGU_COL_BLOCK // SGU_GROUP_DIM):
            grp = cb * (SGU_COL_BLOCK // SGU_GROUP_DIM) + gi
            w_s = ws_ref[grp]
            b_s = bs_ref[grp]
            for c in range(tm // CHUNK):
                rows = slice(c * CHUNK, (c + 1) * CHUNK)
                gcols = slice(grp * SGU_GROUP_DIM, (grp + 1) * SGU_GROUP_DIM)
                sv = jnp.dot(w_s, v_scr[rows, gcols], preferred_element_type=F32) + b_s
                lcols = slice(gi * SGU_GROUP_DIM, (gi + 1) * SGU_GROUP_DIM)
                t_scr[rows, lcols] = (u[rows, lcols] * sv).astype(BF16)
        y = y + jnp.dot(t_scr[...], wout_ref[cols, :], preferred_element_type=F32)
    out_ref[...] = x + gate_ref[...] * y


def _sgu(x2d, tm, mod, mod_tiles_per_row, norm_g, w_in, b_in, v_g, w_s, b_s, w_out):
    t = x2d.shape[0]
    const = lambda i: (0, 0)
    const3 = lambda i: (0, 0, 0)
    tile = pl.BlockSpec((tm, D_MODEL), lambda i: (i, 0))
    return pl.pallas_call(
        _sgu_kernel,
        grid=(t // tm,),
        in_specs=[
            tile,
            pl.BlockSpec((1, D_MODEL), const),
            _mod_spec(0, mod_tiles_per_row),
            _mod_spec(1, mod_tiles_per_row),
            _mod_spec(2, mod_tiles_per_row),
            _resident((D_MODEL, 2 * SGU_DIM), const),
            pl.BlockSpec((1, 2 * SGU_DIM), const),
            pl.BlockSpec((1, SGU_DIM), const),
            pl.BlockSpec((SGU_GROUPS, CHUNK, CHUNK), const3),
            pl.BlockSpec((SGU_GROUPS, CHUNK, 1), const3),
            _resident((SGU_DIM, D_MODEL), const),
        ],
        out_specs=tile,
        out_shape=jax.ShapeDtypeStruct(x2d.shape, F32),
        scratch_shapes=[pltpu.VMEM((tm, SGU_DIM), BF16), pltpu.VMEM((tm, SGU_COL_BLOCK), BF16)],
        compiler_params=_params(1),
        name="sgu",
    )(x2d, norm_g, mod, mod, mod, w_in, b_in, v_g, w_s, b_s, w_out)


def _deinterleave_heads(n_heads):
    half = np.concatenate([np.arange(0, HEAD_DIM, 2), np.arange(1, HEAD_DIM, 2)])
    return (np.arange(n_heads)[:, None] * HEAD_DIM + half[None, :]).reshape(-1)


def kernel(x, c, ctx, c_ctx, ada_w, ada_b, mix_norm_g, mlp_norm_g, mlp_w1, mlp_w2, attn_wqkv,
           attn_q_g, attn_k_g, attn_wo, sgu_w_in, sgu_b_in, sgu_v_g, sgu_w_s, sgu_b_s, sgu_w_out,
           final_g):
    b, n, d = x.shape
    lc = ctx.shape[1]
    tm_x, tm_c = 512, 256
    tq, tk = 1024, 768

    cond = jnp.concatenate([c, c_ctx[None, :], jnp.zeros((MOD_ROWS - b - 1, d), F32)], axis=0)
    mods = _adaln(cond, ada_w, ada_b)
    cos_t, sin_t = _rope_tables(n)
    cos_c = jnp.ones((HEAD_DIM // 2, lc), F32)
    sin_c = jnp.zeros((HEAD_DIM // 2, lc), F32)
    perm = np.concatenate([_deinterleave_heads(N_HEADS),
                           Q_DIM + _deinterleave_heads(N_KV_HEADS),
                           np.arange(Q_DIM + KV_DIM, QKV_DIM)])
    head_perm = _deinterleave_heads(1)

    x2 = x.reshape(b * n, d)
    c2 = ctx.reshape(b * lc, d)
    row = lambda v: v.reshape(1, -1)
    for i in range(DEPTH):
        last = i == DEPTH - 1
        use_attn = (i % N_MIXERS) == 0
        j = i // N_MIXERS
        mod_x = mods[i, :b].reshape(b, 1, N_MOD * d)
        mod_c = mods[i, b:b + 1].reshape(1, 1, N_MOD * d)
        x_tpr, c_tpr = n // tm_x, (b * lc) // tm_c
        mix_g = row(mix_norm_g[i])
        mlp_g = row(mlp_norm_g[i])
        w1 = mlp_w1[i].astype(BF16)
        w2 = mlp_w2[i].astype(BF16)
        fg = row(final_g) if last else None
        if use_attn:
            wqkv_t = attn_wqkv[j][:, perm].T.astype(BF16)
            q_g = attn_q_g[j][head_perm][:, None]
            k_g = attn_k_g[j][head_perm][:, None]
            wo = attn_wo[j].astype(BF16)
            q_t, k_all, vt_all = _qkv(x2, n, tm_x, mod_x, x_tpr, mix_g, wqkv_t, q_g, k_g,
                                      cos_t, sin_t, kv_len=n + lc, kv_offset=0)
            qc_t, k_all, vt_all = _qkv(c2, lc, tm_c, mod_c, c_tpr, mix_g, wqkv_t, q_g, k_g,
                                       cos_c, sin_c, kv_len=n + lc, kv_offset=n,
                                       kv_buffers=(k_all, vt_all))
            o = _flash(q_t, k_all, vt_all, n, tq, tk)
            x2 = _mlp(x2, tm_x, mod_x, x_tpr, mlp_g, w1, w2, attn_o=o, wo=wo, final_g=fg)
            if not last:
                kc, vtc = k_all[:, :, n:], vt_all[:, :, :, n:]
                oc = _flash(qc_t, kc, vtc, lc, lc, lc)
                c2 = _mlp(c2, tm_c, mod_c, c_tpr, mlp_g, w1, w2, attn_o=oc, wo=wo)
        else:
            sgu_args = (sgu_w_in[j].astype(BF16), row(sgu_b_in[j]), row(sgu_v_g[j]),
                        sgu_w_s[j].astype(BF16), sgu_b_s[j][:, :, None],
                        sgu_w_out[j].astype(BF16))
            x2 = _sgu(x2, 2 * CHUNK, mod_x, n // (2 * CHUNK), mix_g, *sgu_args)
            x2 = _mlp(x2, tm_x, mod_x, x_tpr, mlp_g, w1, w2, final_g=fg)
            if not last:
                c2 = _sgu(c2, tm_c, mod_c, c_tpr, mix_g, *sgu_args)
                c2 = _mlp(c2, tm_c, mod_c, c_tpr, mlp_g, w1, w2)
    return x2.reshape(b, n, d)
```

```python
import functools

import jax
import jax.numpy as jnp
import numpy as np
from jax import lax
from jax.experimental import pallas as pl
from jax.experimental.pallas import tpu as pltpu

D_MODEL = 1024
DEPTH = 2
GRID_W = 64
N_MIXERS = 2
N_HEADS = 8
N_KV_HEADS = 2
HEAD_DIM = 128
GROUP = N_HEADS // N_KV_HEADS
Q_DIM = N_HEADS * HEAD_DIM
KV_DIM = N_KV_HEADS * HEAD_DIM
QKV_DIM = Q_DIM + 2 * KV_DIM
ROPE_AXIS_DIM = HEAD_DIM // 2
ROPE_THETA = 10000.0
CHUNK = 128
SGU_DIM = 3 * D_MODEL
SGU_GROUPS = 8
SGU_GROUP_DIM = SGU_DIM // SGU_GROUPS
D_FF = 4 * D_MODEL
N_MOD = 6
EPS = 1e-6

F32 = jnp.float32
BF16 = jnp.bfloat16

VMEM_LIMIT_BYTES = 56 * 1024 * 1024
MOD_ROWS = 8

SOFTMAX_Q_SCALE = float(HEAD_DIM ** -0.5 * np.log2(np.e))
NEG_BIG = -1e30
BF16_SUBLANE_TILE = 16
VT_ROWS = HEAD_DIM + BF16_SUBLANE_TILE


def _params(n_axes):
    return pltpu.CompilerParams(dimension_semantics=("parallel",) * n_axes,
                                vmem_limit_bytes=VMEM_LIMIT_BYTES)


def _resident(shape, index_map):
    return pl.BlockSpec(shape, index_map, pipeline_mode=pl.Buffered(1))


def _rms_mod(x, g, shift, scale):
    y = x * lax.rsqrt(jnp.mean(x * x, axis=-1, keepdims=True) + EPS) * g
    return y * (1.0 + scale) + shift


def _mod_spec(k, tiles_per_row):
    return pl.BlockSpec((None, 1, D_MODEL), lambda i: (i // tiles_per_row, 0, k))


def _adaln_kernel(c_ref, w_ref, b_ref, o_ref):
    c = c_ref[...]
    s = c * jax.nn.sigmoid(c)
    o_ref[...] = jnp.dot(s.astype(BF16), w_ref[...].astype(BF16),
                         preferred_element_type=F32) + b_ref[...]


def _adaln(cond, ada_w, ada_b):
    depth = ada_w.shape[0]
    return pl.pallas_call(
        _adaln_kernel,
        grid=(depth, N_MOD),
        in_specs=[
            pl.BlockSpec((MOD_ROWS, D_MODEL), lambda i, j: (0, 0)),
            pl.BlockSpec((None, D_MODEL, D_MODEL), lambda i, j: (i, 0, j)),
            pl.BlockSpec((None, 1, D_MODEL), lambda i, j: (i, 0, j)),
        ],
        out_specs=pl.BlockSpec((None, MOD_ROWS, D_MODEL), lambda i, j: (i, 0, j)),
        out_shape=jax.ShapeDtypeStruct((depth, MOD_ROWS, N_MOD * D_MODEL), F32),
        compiler_params=_params(2),
        name="adaln",
    )(cond, ada_w, ada_b.reshape(depth, 1, N_MOD * D_MODEL))


def _cos_sin_kernel(ang_ref, cos_ref, sin_ref):
    a = ang_ref[...]
    cos_ref[...] = jnp.cos(a)
    sin_ref[...] = jnp.sin(a)


def _rope_tables(n):
    rows_count = n // GRID_W
    freqs = 1.0 / (ROPE_THETA ** (jnp.arange(0, ROPE_AXIS_DIM, 2, dtype=F32) / ROPE_AXIS_DIM))
    pos = jnp.arange(rows_count + GRID_W, dtype=jnp.int32)
    pos = jnp.where(pos < rows_count, pos, pos - rows_count).astype(F32)
    ang = pos[:, None] * freqs[None, :]
    cos_s, sin_s = pl.pallas_call(
        _cos_sin_kernel,
        out_shape=(jax.ShapeDtypeStruct(ang.shape, F32),) * 2,
        name="rope_cos_sin",
    )(ang)

    def expand(t):
        by_row = jnp.repeat(t[:rows_count], GRID_W, axis=0)
        by_col = jnp.tile(t[rows_count:], (rows_count, 1))
        return jnp.concatenate([by_row, by_col], axis=-1)

    return expand(cos_s).T, expand(sin_s).T


Q_HEADS_PER_DOT = 4


def _qkv_kernel(x_ref, g_ref, shift_ref, scale_ref, wt_ref, qg_ref, kg_ref, cos_ref, sin_ref,
                qt_ref, k_ref, vt_ref):
    h =_rms_mod(x_ref[...], g_ref[...], shift_ref[...], scale_ref[...]).astype(BF16)
    cos = cos_ref[...]
    sin = sin_ref[...]
    half = HEAD_DIM // 2
    nt_dims = (((1,), (1,)), ((), ()))

    def norm_rope(t, g):
        t = t * lax.rsqrt(jnp.mean(t * t, axis=0, keepdims=True) + EPS) * g
        x1, x2 = t[:half], t[half:]
        return x1 * cos - x2 * sin, x1 * sin + x2 * cos

    qg = qg_ref[...]
    for blk in range(N_HEADS // Q_HEADS_PER_DOT):
        rows = slice(blk * Q_HEADS_PER_DOT * HEAD_DIM, (blk + 1) * Q_HEADS_PER_DOT * HEAD_DIM)
        yt = lax.dot_general(wt_ref[rows, :], h, nt_dims, preferred_element_type=F32)
        for j in range(Q_HEADS_PER_DOT):
            r0 = (blk * Q_HEADS_PER_DOT + j) * HEAD_DIM
            o1, o2 = norm_rope(yt[j * HEAD_DIM:(j + 1) * HEAD_DIM], qg)
            qt_ref[r0:r0 + half, :] = (o1 * SOFTMAX_Q_SCALE).astype(BF16)
            qt_ref[r0 + half:r0 + HEAD_DIM, :] = (o2 * SOFTMAX_Q_SCALE).astype(BF16)
    yt = lax.dot_general(wt_ref[Q_DIM:, :], h, nt_dims, preferred_element_type=F32)
    kg = kg_ref[...]
    for hd in range(N_KV_HEADS):
        o1, o2 = norm_rope(yt[hd * HEAD_DIM:(hd + 1) * HEAD_DIM], kg)
        k_ref[hd] = jnp.concatenate([o1, o2], axis=0).T.astype(BF16)
        v_t = yt[KV_DIM + hd * HEAD_DIM:KV_DIM + (hd + 1) * HEAD_DIM]
        vt_ref[hd, :HEAD_DIM, :] = v_t.astype(BF16)
        vt_ref[hd, HEAD_DIM:, :] = jnp.ones((VT_ROWS - HEAD_DIM, v_t.shape[1]), BF16)


def _qkv(x2d, seq, tm, mod, mod_tiles_per_row, norm_g, wqkv_t, q_g, k_g, cos_t, sin_t):
    t = x2d.shape[0]
    tpb = seq // tm
    const = lambda i: (0, 0)
    return pl.pallas_call(
        _qkv_kernel,
        grid=(t // tm,),
        in_specs=[
            pl.BlockSpec((tm, D_MODEL), lambda i: (i, 0)),
            pl.BlockSpec((1, D_MODEL), const),
            _mod_spec(0, mod_tiles_per_row),
            _mod_spec(1, mod_tiles_per_row),
            _resident((QKV_DIM, D_MODEL), const),
            pl.BlockSpec((HEAD_DIM, 1), const),
            pl.BlockSpec((HEAD_DIM, 1), const),
            pl.BlockSpec((HEAD_DIM // 2, tm), lambda i: (0, i % tpb)),
            pl.BlockSpec((HEAD_DIM // 2, tm), lambda i: (0, i % tpb)),
        ],
        out_specs=[
            pl.BlockSpec((Q_DIM, tm), lambda i: (0, i)),
            pl.BlockSpec((None, N_KV_HEADS, tm, HEAD_DIM), lambda i: (i // tpb, 0, i % tpb, 0)),
            pl.BlockSpec((None, N_KV_HEADS, VT_ROWS, tm), lambda i: (i // tpb, 0, 0, i % tpb)),
        ],
        out_shape=[
            jax.ShapeDtypeStruct((Q_DIM, t), BF16),
            jax.ShapeDtypeStruct((t // seq, N_KV_HEADS, seq, HEAD_DIM), BF16),
            jax.ShapeDtypeStruct((t // seq, N_KV_HEADS, VT_ROWS, seq), BF16),
        ],
        compiler_params=_params(1),
        name="qkv_proj",
    )(x2d, norm_g, mod, mod, wqkv_t, q_g, k_g, cos_t, sin_t)


STEPS_PER_LOOP_BODY = 4
L_MIN = 2.0 ** -60


def _flash_kernel(*refs, tiles):
    n_src = len(tiles)
    q_ref = refs[0]
    k_refs = refs[1:1 + 2 * n_src:2]
    vt_refs = refs[2:2 + 2 * n_src:2]
    o_ref, shift_scr, acc_scr, kmax_scr = refs[1 + 2 * n_src:]

    @pl.when(pl.program_id(2) == 0)
    def _():
        kmax2 = jnp.zeros((1, 1), F32)
        for k_ref, (tk, nk) in zip(k_refs, tiles):
            def body(t, n2, k_ref=k_ref, tk=tk):
                kk = k_ref[pl.ds(pl.multiple_of(t * tk, tk), tk), :].astype(F32)
                return jnp.maximum(n2, jnp.sum(kk * kk, axis=1, keepdims=True))

            n2 = lax.fori_loop(0, nk, body, jnp.zeros((tk, 1), F32))
            kmax2 = jnp.maximum(kmax2, jnp.max(n2, axis=0, keepdims=True))
        kmax_scr[...] = jnp.broadcast_to(jnp.sqrt(kmax2), kmax_scr.shape)

    def operands(src, g, t):
        tk = tiles[src][0]
        ks = t * tk if isinstance(t, int) else pl.multiple_of(t * tk, tk)
        qs = g * HEAD_DIM if isinstance(g, int) else pl.multiple_of(g * HEAD_DIM, HEAD_DIM)
        q_t = q_ref[pl.ds(qs, HEAD_DIM), :]
        return k_refs[src][pl.ds(ks, tk), :], q_t, vt_refs[src][:, pl.ds(ks, tk)]

    def finalize():
        l_min = None
        for g in range(GROUP):
            acc = acc_scr[g]
            l = acc[HEAD_DIM:HEAD_DIM + 1]
            o = acc[:HEAD_DIM] * (1.0 / l)
            o_ref[:, g * HEAD_DIM:(g + 1) * HEAD_DIM] = o.T.astype(BF16)
            l_min = l if l_min is None else jnp.minimum(l_min, l)
        return jnp.min(l_min)

    for g in range(GROUP):
        qf = q_ref[g * HEAD_DIM:(g + 1) * HEAD_DIM, :].astype(F32)
        shift_scr[g] = jnp.sqrt(jnp.sum(qf * qf, axis=0, keepdims=True)) * kmax_scr[...]
    acc_scr[...] = jnp.zeros(acc_scr.shape, F32)

    def bounded_step(src, g, t):
        k_tile, q_t, vt_tile = operands(src, g, t)
        s = jnp.dot(k_tile, q_t, preferred_element_type=F32)
        p = jnp.exp2(s - shift_scr[g]).astype(BF16)
        acc_scr[g] += jnp.dot(vt_tile, p, preferred_element_type=F32)

    def steps(src, step_fn):
        nk = tiles[src][1]
        n_steps = GROUP * nk
        n_loops, n_tail = divmod(n_steps, STEPS_PER_LOOP_BODY)

        def body(i, carry):
            for r in range(STEPS_PER_LOOP_BODY):
                u = STEPS_PER_LOOP_BODY * i + r
                g = u // nk
                step_fn(src, g, u - g * nk)
            return carry

        if n_loops:
            lax.fori_loop(0, n_loops, body, 0)
        for u in range(n_steps - n_tail, n_steps):
            step_fn(src, *divmod(u, nk))

    for src in range(n_src):
        steps(src, bounded_step)
    l_min = finalize()

    @pl.when(jnp.logical_not(l_min >= L_MIN))
    def _():
        shift_scr[...] = jnp.full(shift_scr.shape, NEG_BIG, F32)
        acc_scr[...] = jnp.zeros(acc_scr.shape, F32)

        def online_step(src, g, t):
            k_tile, q_t, vt_tile = operands(src, g, t)
            s = jnp.dot(k_tile, q_t, preferred_element_type=F32)
            m_old = shift_scr[g]
            m_new = jnp.maximum(m_old, jnp.max(s, axis=0, keepdims=True))
            shift_scr[g] = m_new
            p = jnp.exp2(s - m_new).astype(BF16)
            pv = jnp.dot(vt_tile, p, preferred_element_type=F32)
            acc_scr[g] = jnp.exp2(m_old - m_new) * acc_scr[g] + pv

        for src in range(n_src):
            nk = tiles[src][1]

            def body(u, carry, src=src, nk=nk):
                g = u // nk
                online_step(src, g, u - g * nk)
                return carry

            lax.fori_loop(0, GROUP * nk, body, 0)
        finalize()


def _flash(q_t, sources, seq_q, tq, tk):
    b = sources[0][0].shape[0]
    nq = seq_q // tq
    gw = GROUP * HEAD_DIM
    kv_args, kv_specs, tiles = [], [], []
    for k, vt in sources:
        lk = k.shape[2]
        rows = min(tk, lk)
        tiles.append((rows, lk // rows))
        kv_args += [k, vt]
        kv_specs += [pl.BlockSpec((None, None, lk, HEAD_DIM), lambda bi, h, i: (bi, h, 0, 0)),
                     pl.BlockSpec((None, None, VT_ROWS, lk), lambda bi, h, i: (bi, h, 0, 0))]
    return pl.pallas_call(
        functools.partial(_flash_kernel, tiles=tuple(tiles)),
        grid=(b, N_KV_HEADS, nq),
        in_specs=[pl.BlockSpec((gw, tq), lambda bi, h, i: (h, bi * nq + i))] + kv_specs,
        out_specs=pl.BlockSpec((tq, gw), lambda bi, h, i: (bi * nq + i, h)),
        out_shape=jax.ShapeDtypeStruct((q_t.shape[1], q_t.shape[0]), BF16),
        scratch_shapes=[pltpu.VMEM((GROUP, 1, tq), F32), pltpu.VMEM((GROUP, VT_ROWS, tq), F32),
                        pltpu.VMEM((1, tq), F32)],
        compiler_params=pltpu.CompilerParams(
            dimension_semantics=("parallel", "parallel", "arbitrary"),
            vmem_limit_bytes=VMEM_LIMIT_BYTES),
        name="flash_attn",
    )(q_t, *kv_args)


def _mlp_kernel(*refs, with_proj, with_final_norm, ff_chunk):
    refs = list(refs)
    x_ref = refs.pop(0)
    if with_proj:
        o_ref, wo_ref, gate_mix_ref = refs.pop(0), refs.pop(0), refs.pop(0)
    g_ref, shift_ref, scale_ref, gate_ref, w1_ref, w2_ref = refs[:6]
    refs = refs[6:]
    if with_final_norm:
        fg_ref = refs.pop(0)
    (out_ref,) = refs

    x = x_ref[...]
    if with_proj:
        x = x + gate_mix_ref[...] * jnp.dot(o_ref[...], wo_ref[...], preferred_element_type=F32)
    h = _rms_mod(x, g_ref[...], shift_ref[...], scale_ref[...]).astype(BF16)
    y = jnp.zeros_like(x)
    for c in range(D_FF // ff_chunk):
        sl = slice(c * ff_chunk, (c + 1) * ff_chunk)
        a = jnp.maximum(jnp.dot(h, w1_ref[:, sl], preferred_element_type=F32), 0.0)
        y = y + jnp.dot((a * a).astype(BF16), w2_ref[sl, :], preferred_element_type=F32)
    x = x + gate_ref[...] * y
    if with_final_norm:
        x = x * lax.rsqrt(jnp.mean(x * x, axis=-1, keepdims=True) + EPS) * fg_ref[...]
    out_ref[...] = x


def _mlp(x2d, tm, mod, mod_tiles_per_row, norm_g, w1, w2, layer, *, attn_o=None, wo=None,
         final_g=None):
    t = x2d.shape[0]
    const = lambda i: (0, 0)
    this_layer = lambda i: (layer, 0, 0)
    tile = pl.BlockSpec((tm, D_MODEL), lambda i: (i, 0))
    args, specs = [x2d], [tile]
    if attn_o is not None:
        args += [attn_o, wo, mod]
        specs += [pl.BlockSpec((tm, Q_DIM), lambda i: (i, 0)), _resident((Q_DIM, D_MODEL), const),
                  _mod_spec(2, mod_tiles_per_row)]
    args += [norm_g, mod, mod, mod, w1, w2]
    specs += [pl.BlockSpec((1, D_MODEL), const), _mod_spec(3, mod_tiles_per_row),
              _mod_spec(4, mod_tiles_per_row), _mod_spec(5, mod_tiles_per_row),
              _resident((None, D_MODEL, D_FF), this_layer),
              _resident((None, D_FF, D_MODEL), this_layer)]
    if final_g is not None:
        args.append(final_g)
        specs.append(pl.BlockSpec((1, D_MODEL), const))
    return pl.pallas_call(
        functools.partial(_mlp_kernel, with_proj=attn_o is not None,
                          with_final_norm=final_g is not None, ff_chunk=1024),
        grid=(t // tm,),
        in_specs=specs,
        out_specs=tile,
        out_shape=jax.ShapeDtypeStruct(x2d.shape, F32),
        compiler_params=_params(1),
        name="mlp",
    )(*args)


def _gelu(z):
    return 0.5 * z * (1.0 + lax.erf(z * float(np.sqrt(0.5))))


SGU_COL_BLOCK = 2 * SGU_GROUP_DIM


def _sgu_kernel(x_ref, g_ref, shift_ref, scale_ref, gate_ref, win_ref, bin_ref, vg_ref, ws_ref,
                bs_ref, wout_ref, out_ref, v_scr, t_scr):
    tm = x_ref.shape[0]
    x = x_ref[...]
    h = _rms_mod(x, g_ref[...], shift_ref[...], scale_ref[...]).astype(BF16)
    zv = _gelu(jnp.dot(h, win_ref[:, SGU_DIM:], preferred_element_type=F32) + bin_ref[:, SGU_DIM:])
    r = lax.rsqrt(jnp.mean(zv * zv, axis=-1, keepdims=True) + EPS)
    v_scr[...] = (zv * r * vg_ref[...]).astype(BF16)
    y = jnp.zeros_like(x)
    for cb in range(SGU_DIM // SGU_COL_BLOCK):
        cols = slice(cb * SGU_COL_BLOCK, (cb + 1) * SGU_COL_BLOCK)
        u = _gelu(jnp.dot(h, win_ref[:, cols], preferred_element_type=F32) + bin_ref[:, cols])
        for gi in range(SGU_COL_BLOCK // SGU_GROUP_DIM):
            grp = cb * (SGU_COL_BLOCK // SGU_GROUP_DIM) + gi
            w_s = ws_ref[grp]
            b_s = bs_ref[grp]
            for c in range(tm // CHUNK):
                rows = slice(c * CHUNK, (c + 1) * CHUNK)
                gcols = slice(grp * SGU_GROUP_DIM, (grp + 1) * SGU_GROUP_DIM)
                sv = jnp.dot(w_s, v_scr[rows, gcols], preferred_element_type=F32) + b_s
                lcols = slice(gi * SGU_GROUP_DIM, (gi + 1) * SGU_GROUP_DIM)
                t_scr[rows, lcols] = (u[rows, lcols] * sv).astype(BF16)
        y = y + jnp.dot(t_scr[...], wout_ref[cols, :], preferred_element_type=F32)
    out_ref[...] = x + gate_ref[...] * y


def _sgu(x2d, tm, mod, mod_tiles_per_row, norm_g, w_in, b_in, v_g, w_s, b_s, w_out):
    t = x2d.shape[0]
    const = lambda i: (0, 0)
    const3 = lambda i: (0, 0, 0)
    tile = pl.BlockSpec((tm, D_MODEL), lambda i: (i, 0))
    return pl.pallas_call(
        _sgu_kernel,
        grid=(t // tm,),
        in_specs=[
            tile,
            pl.BlockSpec((1, D_MODEL), const),
            _mod_spec(0, mod_tiles_per_row),
            _mod_spec(1, mod_tiles_per_row),
            _mod_spec(2, mod_tiles_per_row),
            _resident((D_MODEL, 2 * SGU_DIM), const),
            pl.BlockSpec((1, 2 * SGU_DIM), const),
            pl.BlockSpec((1, SGU_DIM), const),
            pl.BlockSpec((SGU_GROUPS, CHUNK, CHUNK), const3),
            pl.BlockSpec((SGU_GROUPS, CHUNK, 1), const3),
            _resident((SGU_DIM, D_MODEL), const),
        ],
        out_specs=tile,
        out_shape=jax.ShapeDtypeStruct(x2d.shape, F32),
        scratch_shapes=[pltpu.VMEM((tm, SGU_DIM), BF16), pltpu.VMEM((tm, SGU_COL_BLOCK), BF16)],
        compiler_params=_params(1),
        name="sgu",
    )(x2d, norm_g, mod, mod, mod, w_in, b_in, v_g, w_s, b_s, w_out)


def _deinterleave_heads(n_heads):
    half = np.concatenate([np.arange(0, HEAD_DIM, 2), np.arange(1, HEAD_DIM, 2)])
    return (np.arange(n_heads)[:, None] * HEAD_DIM + half[None, :]).reshape(-1)


def kernel(x, c, ctx, c_ctx, ada_w, ada_b, mix_norm_g, mlp_norm_g, mlp_w1, mlp_w2, attn_wqkv,
           attn_q_g, attn_k_g, attn_wo, sgu_w_in, sgu_b_in, sgu_v_g, sgu_w_s, sgu_b_s, sgu_w_out,
           final_g):
    b, n, d = x.shape
    lc = ctx.shape[1]
    tm_x, tm_c = 512, 256
    tq, tk = 1024, 1024

    cond = jnp.concatenate([c, c_ctx[None, :], jnp.zeros((MOD_ROWS - b - 1, d), F32)], axis=0)
    mods = _adaln(cond, ada_w, ada_b)
    cos_t, sin_t = _rope_tables(n)
    cos_c = jnp.ones((HEAD_DIM // 2, lc), F32)
    sin_c = jnp.zeros((HEAD_DIM // 2, lc), F32)
    perm = np.concatenate([_deinterleave_heads(N_HEADS),
                           Q_DIM + _deinterleave_heads(N_KV_HEADS),
                           np.arange(Q_DIM + KV_DIM, QKV_DIM)])
    head_perm = _deinterleave_heads(1)

    x2 = x.reshape(b * n, d)
    c2 = ctx.reshape(b * lc, d)
    row = lambda v: v.reshape(1, -1)
    w1 = mlp_w1.astype(BF16)
    w2 = mlp_w2.astype(BF16)
    for i in range(DEPTH):
        last = i == DEPTH - 1
        use_attn = (i % N_MIXERS) == 0
        j = i // N_MIXERS
        mod_x = mods[i, :b].reshape(b, 1, N_MOD * d)
        mod_c = mods[i, b:b + 1].reshape(1, 1, N_MOD * d)
        x_tpr, c_tpr = n // tm_x, (b * lc) // tm_c
        mix_g = row(mix_norm_g[i])
        mlp_g = row(mlp_norm_g[i])
        fg = row(final_g) if last else None
        if use_attn:
            wqkv_t = attn_wqkv[j][:, perm].T.astype(BF16)
            q_g = attn_q_g[j][head_perm][:, None]
            k_g = attn_k_g[j][head_perm][:, None]
            wo = attn_wo[j].astype(BF16)
            q_t, k, vt = _qkv(x2, n, tm_x, mod_x, x_tpr, mix_g, wqkv_t, q_g, k_g, cos_t, sin_t)
            qc_t, kc, vtc = _qkv(c2, lc, tm_c, mod_c, c_tpr, mix_g, wqkv_t, q_g, k_g,
                                 cos_c, sin_c)
            o = _flash(q_t, [(k, vt), (kc, vtc)], n, tq, tk)
            x2 = _mlp(x2, tm_x, mod_x, x_tpr, mlp_g, w1, w2, i, attn_o=o, wo=wo, final_g=fg)
            if not last:
                oc = _flash(qc_t, [(kc, vtc)], lc, lc, tk)
                c2 = _mlp(c2, tm_c, mod_c, c_tpr, mlp_g, w1, w2, i, attn_o=oc, wo=wo)
        else:
            sgu_args = (sgu_w_in[j].astype(BF16), row(sgu_b_in[j]), row(sgu_v_g[j]),
                        sgu_w_s[j].astype(BF16), sgu_b_s[j][:, :, None],
                        sgu_w_out[j].astype(BF16))
            x2 = _sgu(x2, 2 * CHUNK, mod_x, n // (2 * CHUNK), mix_g, *sgu_args)
            x2 = _mlp(x2, tm_x, mod_x, x_tpr, mlp_g, w1, w2, i, final_g=fg)
            if not last:
                c2 = _sgu(c2, tm_c, mod_c, c_tpr, mix_g, *sgu_args)
                c2 = _mlp(c2, tm_c, mod_c, c_tpr, mlp_g, w1, w2, i)
    return x2.reshape(b, n, d)
```

```python
import functools

import jax
import jax.numpy as jnp
import numpy as np
from jax import lax
from jax.experimental import pallas as pl
from jax.experimental.pallas import tpu as pltpu

D_MODEL = 1024
DEPTH = 2
GRID_W = 64
N_MIXERS = 2
N_HEADS = 8
N_KV_HEADS = 2
HEAD_DIM = 128
GROUP = N_HEADS // N_KV_HEADS
Q_DIM = N_HEADS * HEAD_DIM
KV_DIM = N_KV_HEADS * HEAD_DIM
QKV_DIM = Q_DIM + 2 * KV_DIM
ROPE_AXIS_DIM = HEAD_DIM // 2
ROPE_THETA = 10000.0
CHUNK = 128
SGU_DIM = 3 * D_MODEL
SGU_GROUPS = 8
SGU_GROUP_DIM = SGU_DIM // SGU_GROUPS
D_FF = 4 * D_MODEL
N_MOD = 6
EPS = 1e-6

F32 = jnp.float32
BF16 = jnp.bfloat16

VMEM_LIMIT_BYTES = 56 * 1024 * 1024
MOD_ROWS = 8

SOFTMAX_Q_SCALE = float(HEAD_DIM ** -0.5 * np.log2(np.e))
NEG_BIG = -1e30
BF16_SUBLANE_TILE = 16
VT_ROWS = HEAD_DIM + BF16_SUBLANE_TILE


def _params(n_axes):
    return pltpu.CompilerParams(dimension_semantics=("parallel",) * n_axes,
                                vmem_limit_bytes=VMEM_LIMIT_BYTES)


def _resident(shape, index_map):
    return pl.BlockSpec(shape, index_map, pipeline_mode=pl.Buffered(1))


def _rms_mod(x, g, shift, scale):
    y = x * lax.rsqrt(jnp.mean(x * x, axis=-1, keepdims=True) + EPS) * g
    return y * (1.0 + scale) + shift


def _mod_spec(k, tiles_per_row):
    return pl.BlockSpec((None, 1, D_MODEL), lambda i: (i // tiles_per_row, 0, k))


def _adaln_kernel(c_ref, w_ref, b_ref, o_ref):
    c = c_ref[...]
    s = c * jax.nn.sigmoid(c)
    o_ref[...] = jnp.dot(s.astype(BF16), w_ref[...].astype(BF16),
                         preferred_element_type=F32) + b_ref[...]


def _adaln(cond, ada_w, ada_b):
    depth = ada_w.shape[0]
    return pl.pallas_call(
        _adaln_kernel,
        grid=(depth, N_MOD),
        in_specs=[
            pl.BlockSpec((MOD_ROWS, D_MODEL), lambda i, j: (0, 0)),
            pl.BlockSpec((None, D_MODEL, D_MODEL), lambda i, j: (i, 0, j)),
            pl.BlockSpec((None, 1, D_MODEL), lambda i, j: (i, 0, j)),
        ],
        out_specs=pl.BlockSpec((None, MOD_ROWS, D_MODEL), lambda i, j: (i, 0, j)),
        out_shape=jax.ShapeDtypeStruct((depth, MOD_ROWS, N_MOD * D_MODEL), F32),
        compiler_params=_params(2),
        name="adaln",
    )(cond, ada_w, ada_b.reshape(depth, 1, N_MOD * D_MODEL))


def _cos_sin_kernel(ang_ref, cos_ref, sin_ref):
    a = ang_ref[...]
    cos_ref[...] = jnp.cos(a)
    sin_ref[...] = jnp.sin(a)


def _rope_tables(n):
    rows_count = n // GRID_W
    freqs = 1.0 / (ROPE_THETA ** (jnp.arange(0, ROPE_AXIS_DIM, 2, dtype=F32) / ROPE_AXIS_DIM))
    pos = jnp.arange(rows_count + GRID_W, dtype=jnp.int32)
    pos = jnp.where(pos < rows_count, pos, pos - rows_count).astype(F32)
    ang = pos[:, None] * freqs[None, :]
    cos_s, sin_s = pl.pallas_call(
        _cos_sin_kernel,
        out_shape=(jax.ShapeDtypeStruct(ang.shape, F32),) * 2,
        name="rope_cos_sin",
    )(ang)

    def expand(t):
        by_row = jnp.repeat(t[:rows_count], GRID_W, axis=0)
        by_col = jnp.tile(t[rows_count:], (rows_count, 1))
        return jnp.concatenate([by_row, by_col], axis=-1)

    return expand(cos_s).T, expand(sin_s).T


Q_HEADS_PER_DOT = 4


def _qkv_kernel(x_ref, g_ref, shift_ref, scale_ref, wt_ref, qg_ref, kg_ref, cos_ref, sin_ref,
                qt_ref, k_ref, vt_ref):
    h =_rms_mod(x_ref[...], g_ref[...], shift_ref[...], scale_ref[...]).astype(BF16)
    cos = cos_ref[...]
    sin = sin_ref[...]
    half = HEAD_DIM // 2
    nt_dims = (((1,), (1,)), ((), ()))

    def norm_rope(t, g):
        t = t * lax.rsqrt(jnp.mean(t * t, axis=0, keepdims=True) + EPS) * g
        x1, x2 = t[:half], t[half:]
        return x1 * cos - x2 * sin, x1 * sin + x2 * cos

    qg = qg_ref[...]
    for blk in range(N_HEADS // Q_HEADS_PER_DOT):
        rows = slice(blk * Q_HEADS_PER_DOT * HEAD_DIM, (blk + 1) * Q_HEADS_PER_DOT * HEAD_DIM)
        yt = lax.dot_general(wt_ref[rows, :], h, nt_dims, preferred_element_type=F32)
        for j in range(Q_HEADS_PER_DOT):
            r0 = (blk * Q_HEADS_PER_DOT + j) * HEAD_DIM
            o1, o2 = norm_rope(yt[j * HEAD_DIM:(j + 1) * HEAD_DIM], qg)
            qt_ref[r0:r0 + half, :] = (o1 * SOFTMAX_Q_SCALE).astype(BF16)
            qt_ref[r0 + half:r0 + HEAD_DIM, :] = (o2 * SOFTMAX_Q_SCALE).astype(BF16)
    yt = lax.dot_general(wt_ref[Q_DIM:, :], h, nt_dims, preferred_element_type=F32)
    kg = kg_ref[...]
    for hd in range(N_KV_HEADS):
        o1, o2 = norm_rope(yt[hd * HEAD_DIM:(hd + 1) * HEAD_DIM], kg)
        k_ref[hd] = jnp.concatenate([o1, o2], axis=0).T.astype(BF16)
        v_t = yt[KV_DIM + hd * HEAD_DIM:KV_DIM + (hd + 1) * HEAD_DIM]
        vt_ref[hd, :HEAD_DIM, :] = v_t.astype(BF16)
        vt_ref[hd, HEAD_DIM:, :] = jnp.ones((VT_ROWS - HEAD_DIM, v_t.shape[1]), BF16)


def _qkv(x2d, seq, tm, mod, mod_tiles_per_row, norm_g, wqkv_t, q_g, k_g, cos_t, sin_t):
    t = x2d.shape[0]
    tpb = seq // tm
    const = lambda i: (0, 0)
    return pl.pallas_call(
        _qkv_kernel,
        grid=(t // tm,),
        in_specs=[
            pl.BlockSpec((tm, D_MODEL), lambda i: (i, 0)),
            pl.BlockSpec((1, D_MODEL), const),
            _mod_spec(0, mod_tiles_per_row),
            _mod_spec(1, mod_tiles_per_row),
            _resident((QKV_DIM, D_MODEL), const),
            pl.BlockSpec((HEAD_DIM, 1), const),
            pl.BlockSpec((HEAD_DIM, 1), const),
            pl.BlockSpec((HEAD_DIM // 2, tm), lambda i: (0, i % tpb)),
            pl.BlockSpec((HEAD_DIM // 2, tm), lambda i: (0, i % tpb)),
        ],
        out_specs=[
            pl.BlockSpec((Q_DIM, tm), lambda i: (0, i)),
            pl.BlockSpec((None, N_KV_HEADS, tm, HEAD_DIM), lambda i: (i // tpb, 0, i % tpb, 0)),
            pl.BlockSpec((None, N_KV_HEADS, VT_ROWS, tm), lambda i: (i // tpb, 0, 0, i % tpb)),
        ],
        out_shape=[
            jax.ShapeDtypeStruct((Q_DIM, t), BF16),
            jax.ShapeDtypeStruct((t // seq, N_KV_HEADS, seq, HEAD_DIM), BF16),
            jax.ShapeDtypeStruct((t // seq, N_KV_HEADS, VT_ROWS, seq), BF16),
        ],
        compiler_params=_params(1),
        name="qkv_proj",
    )(x2d, norm_g, mod, mod, wqkv_t, q_g, k_g, cos_t, sin_t)


STEPS_PER_LOOP_BODY = 4
L_MIN = 2.0 ** -60


def _flash_kernel(*refs, tiles, n_cast):
    n_src = len(tiles)
    n_in = 1 + 2 * n_src + n_cast
    q_ref = refs[0]
    k_refs = refs[1:1 + 2 * n_src:2]
    vt_refs = refs[2:2 + 2 * n_src:2]
    o_ref = refs[n_in]
    shift_scr, acc_scr, kmax_scr = refs[n_in + 1 + n_cast:]

    for w_ref, w_bf16_ref in zip(refs[n_in - n_cast:n_in], refs[n_in + 1:n_in + 1 + n_cast]):
        w_bf16_ref[...] = w_ref[...].astype(BF16)

    @pl.when(pl.program_id(2) == 0)
    def _():
        kmax2 = jnp.zeros((1, 1), F32)
        for k_ref, (tk, nk) in zip(k_refs, tiles):
            def body(t, n2, k_ref=k_ref, tk=tk):
                kk = k_ref[pl.ds(pl.multiple_of(t * tk, tk), tk), :].astype(F32)
                return jnp.maximum(n2, jnp.sum(kk * kk, axis=1, keepdims=True))

            n2 = lax.fori_loop(0, nk, body, jnp.zeros((tk, 1), F32))
            kmax2 = jnp.maximum(kmax2, jnp.max(n2, axis=0, keepdims=True))
        kmax_scr[...] = jnp.broadcast_to(jnp.sqrt(kmax2), kmax_scr.shape)

    def operands(src, g, t):
        tk = tiles[src][0]
        ks = t * tk if isinstance(t, int) else pl.multiple_of(t * tk, tk)
        qs = g * HEAD_DIM if isinstance(g, int) else pl.multiple_of(g * HEAD_DIM, HEAD_DIM)
        q_t = q_ref[pl.ds(qs, HEAD_DIM), :]
        return k_refs[src][pl.ds(ks, tk), :], q_t, vt_refs[src][:, pl.ds(ks, tk)]

    def finalize():
        l_min = None
        for g in range(GROUP):
            acc = acc_scr[g]
            l = acc[HEAD_DIM:HEAD_DIM + 1]
            o = acc[:HEAD_DIM] * (1.0 / l)
            o_ref[:, g * HEAD_DIM:(g + 1) * HEAD_DIM] = o.T.astype(BF16)
            l_min = l if l_min is None else jnp.minimum(l_min, l)
        return jnp.min(l_min)

    for g in range(GROUP):
        qf = q_ref[g * HEAD_DIM:(g + 1) * HEAD_DIM, :].astype(F32)
        shift_scr[g] = jnp.sqrt(jnp.sum(qf * qf, axis=0, keepdims=True)) * kmax_scr[...]
    acc_scr[...] = jnp.zeros(acc_scr.shape, F32)

    def bounded_step(src, g, t):
        k_tile, q_t, vt_tile = operands(src, g, t)
        s = jnp.dot(k_tile, q_t, preferred_element_type=F32)
        p = jnp.exp2(s - shift_scr[g]).astype(BF16)
        acc_scr[g] += jnp.dot(vt_tile, p, preferred_element_type=F32)

    def steps(src, step_fn):
        nk = tiles[src][1]
        n_steps = GROUP * nk
        n_loops, n_tail = divmod(n_steps, STEPS_PER_LOOP_BODY)

        def body(i, carry):
            for r in range(STEPS_PER_LOOP_BODY):
                u = STEPS_PER_LOOP_BODY * i + r
                g = u // nk
                step_fn(src, g, u - g * nk)
            return carry

        if n_loops:
            lax.fori_loop(0, n_loops, body, 0)
        for u in range(n_steps - n_tail, n_steps):
            step_fn(src, *divmod(u, nk))

    for src in range(n_src):
        steps(src, bounded_step)
    l_min = finalize()

    @pl.when(jnp.logical_not(l_min >= L_MIN))
    def _():
        shift_scr[...] = jnp.full(shift_scr.shape, NEG_BIG, F32)
        acc_scr[...] = jnp.zeros(acc_scr.shape, F32)

        def online_step(src, g, t):
            k_tile, q_t, vt_tile = operands(src, g, t)
            s = jnp.dot(k_tile, q_t, preferred_element_type=F32)
            m_old = shift_scr[g]
            m_new = jnp.maximum(m_old, jnp.max(s, axis=0, keepdims=True))
            shift_scr[g] = m_new
            p = jnp.exp2(s - m_new).astype(BF16)
            pv = jnp.dot(vt_tile, p, preferred_element_type=F32)
            acc_scr[g] = jnp.exp2(m_old - m_new) * acc_scr[g] + pv

        for src in range(n_src):
            nk = tiles[src][1]

            def body(u, carry, src=src, nk=nk):
                g = u // nk
                online_step(src, g, u - g * nk)
                return carry

            lax.fori_loop(0, GROUP * nk, body, 0)
        finalize()


def _flash(q_t, sources, seq_q, tq, tk, cast=()):
    b = sources[0][0].shape[0]
    nq = seq_q // tq
    gw = GROUP * HEAD_DIM
    n_grid = b * N_KV_HEADS * nq
    grid_pos = lambda bi, h, i: ((bi * N_KV_HEADS + h) * nq + i, 0)
    cast_specs = []
    for w in cast:
        rows, rem = divmod(w.shape[0], n_grid)
        assert rem == 0 and rows % BF16_SUBLANE_TILE == 0, w.shape
        cast_specs.append(pl.BlockSpec((rows, w.shape[1]), grid_pos))
    kv_args, kv_specs, tiles = [], [], []
    for k, vt in sources:
        lk = k.shape[2]
        rows = min(tk, lk)
        tiles.append((rows, lk // rows))
        kv_args += [k, vt]
        kv_specs += [pl.BlockSpec((None, None, lk, HEAD_DIM), lambda bi, h, i: (bi, h, 0, 0)),
                     pl.BlockSpec((None, None, VT_ROWS, lk), lambda bi, h, i: (bi, h, 0, 0))]
    out = pl.pallas_call(
        functools.partial(_flash_kernel, tiles=tuple(tiles), n_cast=len(cast)),
        grid=(b, N_KV_HEADS, nq),
        in_specs=([pl.BlockSpec((gw, tq), lambda bi, h, i: (h, bi * nq + i))] + kv_specs
                  + cast_specs),
        out_specs=[pl.BlockSpec((tq, gw), lambda bi, h, i: (bi * nq + i, h))] + cast_specs,
        out_shape=([jax.ShapeDtypeStruct((q_t.shape[1], q_t.shape[0]), BF16)]
                   + [jax.ShapeDtypeStruct(w.shape, BF16) for w in cast]),
        scratch_shapes=[pltpu.VMEM((GROUP, 1, tq), F32), pltpu.VMEM((GROUP, VT_ROWS, tq), F32),
                        pltpu.VMEM((1, tq), F32)],
        compiler_params=pltpu.CompilerParams(
            dimension_semantics=("parallel", "parallel", "arbitrary"),
            vmem_limit_bytes=VMEM_LIMIT_BYTES),
        name="flash_attn",
    )(q_t, *kv_args, *cast)
    return out[0], out[1:]


def _mlp_kernel(*refs, with_proj, with_final_norm, ff_chunk):
    refs = list(refs)
    x_ref = refs.pop(0)
    if with_proj:
        o_ref, wo_ref, gate_mix_ref = refs.pop(0), refs.pop(0), refs.pop(0)
    g_ref, shift_ref, scale_ref, gate_ref, w1_ref, w2_ref = refs[:6]
    refs = refs[6:]
    if with_final_norm:
        fg_ref = refs.pop(0)
    (out_ref,) = refs

    x = x_ref[...]
    if with_proj:
        x = x + gate_mix_ref[...] * jnp.dot(o_ref[...], wo_ref[...], preferred_element_type=F32)
    h = _rms_mod(x, g_ref[...], shift_ref[...], scale_ref[...]).astype(BF16)
    y = jnp.zeros_like(x)
    for c in range(D_FF // ff_chunk):
        sl = slice(c * ff_chunk, (c + 1) * ff_chunk)
        a = jnp.maximum(jnp.dot(h, w1_ref[:, sl], preferred_element_type=F32), 0.0)
        y = y + jnp.dot((a * a).astype(BF16), w2_ref[sl, :], preferred_element_type=F32)
    x = x + gate_ref[...] * y
    if with_final_norm:
        x = x * lax.rsqrt(jnp.mean(x * x, axis=-1, keepdims=True) + EPS) * fg_ref[...]
    out_ref[...] = x


def _mlp(x2d, tm, mod, mod_tiles_per_row, norm_g, w1, w2, layer, *, attn_o=None, wo=None,
         final_g=None):
    t = x2d.shape[0]
    const = lambda i: (0, 0)
    this_layer = lambda i: (layer, 0, 0)
    tile = pl.BlockSpec((tm, D_MODEL), lambda i: (i, 0))
    args, specs = [x2d], [tile]
    if attn_o is not None:
        args += [attn_o, wo, mod]
        specs += [pl.BlockSpec((tm, Q_DIM), lambda i: (i, 0)), _resident((Q_DIM, D_MODEL), const),
                  _mod_spec(2, mod_tiles_per_row)]
    args += [norm_g, mod, mod, mod, w1, w2]
    specs += [pl.BlockSpec((1, D_MODEL), const), _mod_spec(3, mod_tiles_per_row),
              _mod_spec(4, mod_tiles_per_row), _mod_spec(5, mod_tiles_per_row),
              _resident((None, D_MODEL, D_FF), this_layer),
              _resident((None, D_FF, D_MODEL), this_layer)]
    if final_g is not None:
        args.append(final_g)
        specs.append(pl.BlockSpec((1, D_MODEL), const))
    return pl.pallas_call(
        functools.partial(_mlp_kernel, with_proj=attn_o is not None,
                          with_final_norm=final_g is not None, ff_chunk=1024),
        grid=(t // tm,),
        in_specs=specs,
        out_specs=tile,
        out_shape=jax.ShapeDtypeStruct(x2d.shape, F32),
        compiler_params=_params(1),
        name="mlp",
    )(*args)


def _gelu(z):
    return 0.5 * z * (1.0 + lax.erf(z * float(np.sqrt(0.5))))


SGU_COL_BLOCK = 2 * SGU_GROUP_DIM


def _sgu_kernel(x_ref, g_ref, shift_ref, scale_ref, gate_ref, win_ref, bin_ref, vg_ref, ws_ref,
                bs_ref, wout_ref, out_ref, v_scr, t_scr):
    tm = x_ref.shape[0]
    x = x_ref[...]
    h = _rms_mod(x, g_ref[...], shift_ref[...], scale_ref[...]).astype(BF16)
    zv = _gelu(jnp.dot(h, win_ref[:, SGU_DIM:], preferred_element_type=F32) + bin_ref[:, SGU_DIM:])
    r = lax.rsqrt(jnp.mean(zv * zv, axis=-1, keepdims=True) + EPS)
    v_scr[...] = (zv * r * vg_ref[...]).astype(BF16)
    y = jnp.zeros_like(x)
    for cb in range(SGU_DIM // SGU_COL_BLOCK):
        cols = slice(cb * SGU_COL_BLOCK, (cb + 1) * SGU_COL_BLOCK)
        u = _gelu(jnp.dot(h, win_ref[:, cols], preferred_element_type=F32) + bin_ref[:, cols])
        for gi in range(SGU_COL_BLOCK // SGU_GROUP_DIM):
            grp = cb * (SGU_COL_BLOCK // SGU_GROUP_DIM) + gi
            w_s = ws_ref[grp]
            b_s = bs_ref[grp]
            for c in range(tm // CHUNK):
                rows = slice(c * CHUNK, (c + 1) * CHUNK)
                gcols = slice(grp * SGU_GROUP_DIM, (grp + 1) * SGU_GROUP_DIM)
                sv = jnp.dot(w_s, v_scr[rows, gcols], preferred_element_type=F32) + b_s
                lcols = slice(gi * SGU_GROUP_DIM, (gi + 1) * SGU_GROUP_DIM)
                t_scr[rows, lcols] = (u[rows, lcols] * sv).astype(BF16)
        y = y + jnp.dot(t_scr[...], wout_ref[cols, :], preferred_element_type=F32)
    out_ref[...] = x + gate_ref[...] * y


def _sgu(x2d, tm, mod, mod_tiles_per_row, norm_g, w_in, b_in, v_g, w_s, b_s, w_out):
    t = x2d.shape[0]
    const = lambda i: (0, 0)
    const3 = lambda i: (0, 0, 0)
    tile = pl.BlockSpec((tm, D_MODEL), lambda i: (i, 0))
    return pl.pallas_call(
        _sgu_kernel,
        grid=(t // tm,),
        in_specs=[
            tile,
            pl.BlockSpec((1, D_MODEL), const),
            _mod_spec(0, mod_tiles_per_row),
            _mod_spec(1, mod_tiles_per_row),
            _mod_spec(2, mod_tiles_per_row),
            _resident((D_MODEL, 2 * SGU_DIM), const),
            pl.BlockSpec((1, 2 * SGU_DIM), const),
            pl.BlockSpec((1, SGU_DIM), const),
            pl.BlockSpec((SGU_GROUPS, CHUNK, CHUNK), const3),
            pl.BlockSpec((SGU_GROUPS, CHUNK, 1), const3),
            _resident((SGU_DIM, D_MODEL), const),
        ],
        out_specs=tile,
        out_shape=jax.ShapeDtypeStruct(x2d.shape, F32),
        scratch_shapes=[pltpu.VMEM((tm, SGU_DIM), BF16), pltpu.VMEM((tm, SGU_COL_BLOCK), BF16)],
        compiler_params=_params(1),
        name="sgu",
    )(x2d, norm_g, mod, mod, mod, w_in, b_in, v_g, w_s, b_s, w_out)


def _deinterleave_heads(n_heads):
    half = np.concatenate([np.arange(0, HEAD_DIM, 2), np.arange(1, HEAD_DIM, 2)])
    return (np.arange(n_heads)[:, None] * HEAD_DIM + half[None, :]).reshape(-1)


def kernel(x, c, ctx, c_ctx, ada_w, ada_b, mix_norm_g, mlp_norm_g, mlp_w1, mlp_w2, attn_wqkv,
           attn_q_g, attn_k_g, attn_wo, sgu_w_in, sgu_b_in, sgu_v_g, sgu_w_s, sgu_b_s, sgu_w_out,
           final_g):
    b, n, d = x.shape
    lc = ctx.shape[1]
    tm_x, tm_c = 512, 256
    tq, tk = 1024, 1024

    cond = jnp.concatenate([c, c_ctx[None, :], jnp.zeros((MOD_ROWS - b - 1, d), F32)], axis=0)
    mods = _adaln(cond, ada_w, ada_b)
    cos_t, sin_t = _rope_tables(n)
    cos_c = jnp.ones((HEAD_DIM // 2, lc), F32)
    sin_c = jnp.zeros((HEAD_DIM // 2, lc), F32)
    perm = np.concatenate([_deinterleave_heads(N_HEADS),
                           Q_DIM + _deinterleave_heads(N_KV_HEADS),
                           np.arange(Q_DIM + KV_DIM, QKV_DIM)])
    head_perm = _deinterleave_heads(1)

    x2 = x.reshape(b * n, d)
    c2 = ctx.reshape(b * lc, d)
    row = lambda v: v.reshape(1, -1)
    stacked = dict(w1=mlp_w1, w2=mlp_w2, wo=attn_wo, w_in=sgu_w_in, w_out=sgu_w_out)
    bf16 = None
    for i in range(DEPTH):
        last = i == DEPTH - 1
        use_attn = (i % N_MIXERS) == 0
        j = i // N_MIXERS
        mod_x = mods[i, :b].reshape(b, 1, N_MOD * d)
        mod_c = mods[i, b:b + 1].reshape(1, 1, N_MOD * d)
        x_tpr, c_tpr = n // tm_x, (b * lc) // tm_c
        mix_g = row(mix_norm_g[i])
        mlp_g = row(mlp_norm_g[i])
        fg = row(final_g) if last else None
        if use_attn:
            wqkv_t = attn_wqkv[j][:, perm].T.astype(BF16)
            q_g = attn_q_g[j][head_perm][:, None]
            k_g = attn_k_g[j][head_perm][:, None]
            q_t, k, vt = _qkv(x2, n, tm_x, mod_x, x_tpr, mix_g, wqkv_t, q_g, k_g, cos_t, sin_t)
            qc_t, kc, vtc = _qkv(c2, lc, tm_c, mod_c, c_tpr, mix_g, wqkv_t, q_g, k_g,
                                 cos_c, sin_c)
            to_cast = () if bf16 else tuple(w.reshape(-1, w.shape[-1]) for w in stacked.values())
            o, casted = _flash(q_t, [(k, vt), (kc, vtc)], n, tq, tk, cast=to_cast)
            if not bf16:
                bf16 = {name: wb.reshape(w.shape)
                        for (name, w), wb in zip(stacked.items(), casted)}
            w1, w2, wo = bf16["w1"], bf16["w2"], bf16["wo"][j]
            x2 = _mlp(x2, tm_x, mod_x, x_tpr, mlp_g, w1, w2, i, attn_o=o, wo=wo, final_g=fg)
            if not last:
                oc, _ = _flash(qc_t, [(kc, vtc)], lc, lc, tk)
                c2 = _mlp(c2, tm_c, mod_c, c_tpr, mlp_g, w1, w2, i, attn_o=oc, wo=wo)
        else:
            w1, w2 = bf16["w1"], bf16["w2"]
            sgu_args = (bf16["w_in"][j], row(sgu_b_in[j]), row(sgu_v_g[j]),
                        sgu_w_s[j].astype(BF16), sgu_b_s[j][:, :, None], bf16["w_out"][j])
            x2 = _sgu(x2, 4 * CHUNK, mod_x, n // (4 * CHUNK), mix_g, *sgu_args)
            x2 = _mlp(x2, tm_x, mod_x, x_tpr, mlp_g, w1, w2, i, final_g=fg)
            if not last:
                c2 = _sgu(c2, tm_c, mod_c, c_tpr, mix_g, *sgu_args)
                c2 = _mlp(c2, tm_c, mod_c, c_tpr, mlp_g, w1, w2, i)
    return x2.reshape(b, n, d)
```

```python
import functools

import jax
import jax.numpy as jnp
import numpy as np
from jax import lax
from jax.experimental import pallas as pl
from jax.experimental.pallas import tpu as pltpu

D_MODEL = 1024
DEPTH = 2
GRID_W = 64
N_MIXERS = 2
N_HEADS = 8
N_KV_HEADS = 2
HEAD_DIM = 128
GROUP = N_HEADS // N_KV_HEADS
Q_DIM = N_HEADS * HEAD_DIM
KV_DIM = N_KV_HEADS * HEAD_DIM
QKV_DIM = Q_DIM + 2 * KV_DIM
ROPE_AXIS_DIM = HEAD_DIM // 2
ROPE_THETA = 10000.0
CHUNK = 128
SGU_DIM = 3 * D_MODEL
SGU_GROUPS = 8
SGU_GROUP_DIM = SGU_DIM // SGU_GROUPS
D_FF = 4 * D_MODEL
N_MOD = 6
EPS = 1e-6

F32 = jnp.float32
BF16 = jnp.bfloat16

VMEM_LIMIT_BYTES = 56 * 1024 * 1024
MOD_ROWS = 8

SOFTMAX_Q_SCALE = float(HEAD_DIM ** -0.5 * np.log2(np.e))
NEG_BIG = -1e30
BF16_SUBLANE_TILE = 16
VT_ROWS = HEAD_DIM + BF16_SUBLANE_TILE


def _params(n_axes):
    return pltpu.CompilerParams(dimension_semantics=("parallel",) * n_axes,
                                vmem_limit_bytes=VMEM_LIMIT_BYTES)


def _resident(shape, index_map):
    return pl.BlockSpec(shape, index_map, pipeline_mode=pl.Buffered(1))


def _rms_mod(x, g, shift, scale):
    y = x * lax.rsqrt(jnp.mean(x * x, axis=-1, keepdims=True) + EPS) * g
    return y * (1.0 + scale) + shift


def _mod_spec(k, tiles_per_row):
    return pl.BlockSpec((None, 1, D_MODEL), lambda i: (i // tiles_per_row, 0, k))


def _adaln_kernel(c_ref, w_ref, b_ref, o_ref):
    c = c_ref[...]
    s = c * jax.nn.sigmoid(c)
    o_ref[...] = jnp.dot(s.astype(BF16), w_ref[...].astype(BF16),
                         preferred_element_type=F32) + b_ref[...]


def _adaln(cond, ada_w, ada_b):
    depth = ada_w.shape[0]
    return pl.pallas_call(
        _adaln_kernel,
        grid=(depth, N_MOD),
        in_specs=[
            pl.BlockSpec((MOD_ROWS, D_MODEL), lambda i, j: (0, 0)),
            pl.BlockSpec((None, D_MODEL, D_MODEL), lambda i, j: (i, 0, j)),
            pl.BlockSpec((None, 1, D_MODEL), lambda i, j: (i, 0, j)),
        ],
        out_specs=pl.BlockSpec((None, MOD_ROWS, D_MODEL), lambda i, j: (i, 0, j)),
        out_shape=jax.ShapeDtypeStruct((depth, MOD_ROWS, N_MOD * D_MODEL), F32),
        compiler_params=_params(2),
        name="adaln",
    )(cond, ada_w, ada_b.reshape(depth, 1, N_MOD * D_MODEL))


def _cos_sin_kernel(ang_ref, cos_ref, sin_ref):
    a = ang_ref[...]
    cos_ref[...] = jnp.cos(a)
    sin_ref[...] = jnp.sin(a)


def _rope_tables(n):
    rows_count = n // GRID_W
    freqs = 1.0 / (ROPE_THETA ** (jnp.arange(0, ROPE_AXIS_DIM, 2, dtype=F32) / ROPE_AXIS_DIM))
    pos = jnp.arange(rows_count + GRID_W, dtype=jnp.int32)
    pos = jnp.where(pos < rows_count, pos, pos - rows_count).astype(F32)
    ang = pos[:, None] * freqs[None, :]
    cos_s, sin_s = pl.pallas_call(
        _cos_sin_kernel,
        out_shape=(jax.ShapeDtypeStruct(ang.shape, F32),) * 2,
        name="rope_cos_sin",
    )(ang)

    def expand(t):
        by_row = jnp.repeat(t[:rows_count], GRID_W, axis=0)
        by_col = jnp.tile(t[rows_count:], (rows_count, 1))
        return jnp.concatenate([by_row, by_col], axis=-1)

    return expand(cos_s).T, expand(sin_s).T


Q_HEADS_PER_DOT = 4


def _qkv_kernel(x_ref, g_ref, shift_ref, scale_ref, wt_ref, qg_ref, kg_ref, cos_ref, sin_ref,
                qt_ref, k_ref, vt_ref):
    h =_rms_mod(x_ref[...], g_ref[...], shift_ref[...], scale_ref[...]).astype(BF16)
    cos = cos_ref[...]
    sin = sin_ref[...]
    half = HEAD_DIM // 2
    nt_dims = (((1,), (1,)), ((), ()))

    def norm_rope(t, g):
        t = t * lax.rsqrt(jnp.mean(t * t, axis=0, keepdims=True) + EPS) * g
        x1, x2 = t[:half], t[half:]
        return x1 * cos - x2 * sin, x1 * sin + x2 * cos

    qg = qg_ref[...]
    for blk in range(N_HEADS // Q_HEADS_PER_DOT):
        rows = slice(blk * Q_HEADS_PER_DOT * HEAD_DIM, (blk + 1) * Q_HEADS_PER_DOT * HEAD_DIM)
        yt = lax.dot_general(wt_ref[rows, :], h, nt_dims, preferred_element_type=F32)
        for j in range(Q_HEADS_PER_DOT):
            r0 = (blk * Q_HEADS_PER_DOT + j) * HEAD_DIM
            o1, o2 = norm_rope(yt[j * HEAD_DIM:(j + 1) * HEAD_DIM], qg)
            qt_ref[r0:r0 + half, :] = (o1 * SOFTMAX_Q_SCALE).astype(BF16)
            qt_ref[r0 + half:r0 + HEAD_DIM, :] = (o2 * SOFTMAX_Q_SCALE).astype(BF16)
    yt = lax.dot_general(wt_ref[Q_DIM:, :], h, nt_dims, preferred_element_type=F32)
    kg = kg_ref[...]
    for hd in range(N_KV_HEADS):
        o1, o2 = norm_rope(yt[hd * HEAD_DIM:(hd + 1) * HEAD_DIM], kg)
        k_ref[hd] = jnp.concatenate([o1, o2], axis=0).T.astype(BF16)
        v_t = yt[KV_DIM + hd * HEAD_DIM:KV_DIM + (hd + 1) * HEAD_DIM]
        vt_ref[hd, :HEAD_DIM, :] = v_t.astype(BF16)
        vt_ref[hd, HEAD_DIM:, :] = jnp.ones((VT_ROWS - HEAD_DIM, v_t.shape[1]), BF16)


def _qkv(x2d, seq, tm, mod, mod_tiles_per_row, norm_g, wqkv_t, q_g, k_g, cos_t, sin_t):
    t = x2d.shape[0]
    tpb = seq // tm
    const = lambda i: (0, 0)
    return pl.pallas_call(
        _qkv_kernel,
        grid=(t // tm,),
        in_specs=[
            pl.BlockSpec((tm, D_MODEL), lambda i: (i, 0)),
            pl.BlockSpec((1, D_MODEL), const),
            _mod_spec(0, mod_tiles_per_row),
            _mod_spec(1, mod_tiles_per_row),
            _resident((QKV_DIM, D_MODEL), const),
            pl.BlockSpec((HEAD_DIM, 1), const),
            pl.BlockSpec((HEAD_DIM, 1), const),
            pl.BlockSpec((HEAD_DIM // 2, tm), lambda i: (0, i % tpb)),
            pl.BlockSpec((HEAD_DIM // 2, tm), lambda i: (0, i % tpb)),
        ],
        out_specs=[
            pl.BlockSpec((Q_DIM, tm), lambda i: (0, i)),
            pl.BlockSpec((None, N_KV_HEADS, tm, HEAD_DIM), lambda i: (i // tpb, 0, i % tpb, 0)),
            pl.BlockSpec((None, N_KV_HEADS, VT_ROWS, tm), lambda i: (i // tpb, 0, 0, i % tpb)),
        ],
        out_shape=[
            jax.ShapeDtypeStruct((Q_DIM, t), BF16),
            jax.ShapeDtypeStruct((t // seq, N_KV_HEADS, seq, HEAD_DIM), BF16),
            jax.ShapeDtypeStruct((t // seq, N_KV_HEADS, VT_ROWS, seq), BF16),
        ],
        compiler_params=_params(1),
        name="qkv_proj",
    )(x2d, norm_g, mod, mod, wqkv_t, q_g, k_g, cos_t, sin_t)


STEPS_PER_LOOP_BODY = 8
L_MIN = 2.0 ** -60


def _flash_kernel(*refs, tiles, n_cast):
    n_src = len(tiles)
    n_in = 1 + 2 * n_src + n_cast
    q_ref = refs[0]
    k_refs = refs[1:1 + 2 * n_src:2]
    vt_refs = refs[2:2 + 2 * n_src:2]
    o_ref = refs[n_in]
    shift_scr, acc_scr, kmax_scr = refs[n_in + 1 + n_cast:]

    for w_ref, w_bf16_ref in zip(refs[n_in - n_cast:n_in], refs[n_in + 1:n_in + 1 + n_cast]):
        w_bf16_ref[...] = w_ref[...].astype(BF16)

    @pl.when(pl.program_id(2) == 0)
    def _():
        kmax2 = jnp.zeros((1, 1), F32)
        for k_ref, (tk, nk) in zip(k_refs, tiles):
            def body(t, n2, k_ref=k_ref, tk=tk):
                kk = k_ref[pl.ds(pl.multiple_of(t * tk, tk), tk), :].astype(F32)
                return jnp.maximum(n2, jnp.sum(kk * kk, axis=1, keepdims=True))

            n2 = lax.fori_loop(0, nk, body, jnp.zeros((tk, 1), F32))
            kmax2 = jnp.maximum(kmax2, jnp.max(n2, axis=0, keepdims=True))
        kmax_scr[...] = jnp.broadcast_to(jnp.sqrt(kmax2), kmax_scr.shape)

    def operands(src, g, t):
        tk = tiles[src][0]
        ks = t * tk if isinstance(t, int) else pl.multiple_of(t * tk, tk)
        qs = g * HEAD_DIM if isinstance(g, int) else pl.multiple_of(g * HEAD_DIM, HEAD_DIM)
        q_t = q_ref[pl.ds(qs, HEAD_DIM), :]
        return k_refs[src][pl.ds(ks, tk), :], q_t, vt_refs[src][:, pl.ds(ks, tk)]

    def finalize():
        l_min = None
        for g in range(GROUP):
            acc = acc_scr[g]
            l = acc[HEAD_DIM:HEAD_DIM + 1]
            o = acc[:HEAD_DIM] * (1.0 / l)
            o_ref[:, g * HEAD_DIM:(g + 1) * HEAD_DIM] = o.astype(BF16).T
            l_min = l if l_min is None else jnp.minimum(l_min, l)
        return jnp.min(l_min)

    for g in range(GROUP):
        qf = q_ref[g * HEAD_DIM:(g + 1) * HEAD_DIM, :].astype(F32)
        shift_scr[g] = jnp.sqrt(jnp.sum(qf * qf, axis=0, keepdims=True)) * kmax_scr[...]

    tiles_static = [STEPS_PER_LOOP_BODY % nk == 0 for _, nk in tiles]
    if not tiles_static[0]:
        acc_scr[...] = jnp.zeros(acc_scr.shape, F32)

    def bounded_step(src, g, t):
        k_tile, q_t, vt_tile = operands(src, g, t)
        s = jnp.dot(k_tile, q_t, preferred_element_type=F32)
        p = jnp.exp2(s - shift_scr[g]).astype(BF16)
        pv = jnp.dot(vt_tile, p, preferred_element_type=F32)
        if tiles_static[0] and src == 0 and t == 0:
            acc_scr[g] = pv
        else:
            acc_scr[g] += pv

    def steps(src, step_fn):
        nk = tiles[src][1]
        n_steps = GROUP * nk
        n_loops, n_tail = divmod(n_steps, STEPS_PER_LOOP_BODY)

        def body(i, carry):
            for r in range(STEPS_PER_LOOP_BODY):
                if tiles_static[src]:
                    g, t = (STEPS_PER_LOOP_BODY // nk) * i + r // nk, r % nk
                else:
                    u = STEPS_PER_LOOP_BODY * i + r
                    g = u // nk
                    t = u - g * nk
                step_fn(src, g, t)
            return carry

        if n_loops:
            lax.fori_loop(0, n_loops, body, 0)
        for u in range(n_steps - n_tail, n_steps):
            step_fn(src, *divmod(u, nk))

    for src in range(n_src):
        steps(src, bounded_step)
    l_min = finalize()

    @pl.when(jnp.logical_not(l_min >= L_MIN))
    def _():
        shift_scr[...] = jnp.full(shift_scr.shape, NEG_BIG, F32)
        acc_scr[...] = jnp.zeros(acc_scr.shape, F32)

        def online_step(src, g, t):
            k_tile, q_t, vt_tile = operands(src, g, t)
            s = jnp.dot(k_tile, q_t, preferred_element_type=F32)
            m_old = shift_scr[g]
            m_new = jnp.maximum(m_old, jnp.max(s, axis=0, keepdims=True))
            shift_scr[g] = m_new
            p = jnp.exp2(s - m_new).astype(BF16)
            pv = jnp.dot(vt_tile, p, preferred_element_type=F32)
            acc_scr[g] = jnp.exp2(m_old - m_new) * acc_scr[g] + pv

        for src in range(n_src):
            nk = tiles[src][1]

            def body(u, carry, src=src, nk=nk):
                g = u // nk
                online_step(src, g, u - g * nk)
                return carry

            lax.fori_loop(0, GROUP * nk, body, 0)
        finalize()


def _flash(q_t, sources, seq_q, tq, tk, cast=()):
    b = sources[0][0].shape[0]
    nq = seq_q // tq
    gw = GROUP * HEAD_DIM
    n_grid = b * N_KV_HEADS * nq
    grid_pos = lambda bi, h, i: ((bi * N_KV_HEADS + h) * nq + i, 0)
    cast_specs = []
    for w in cast:
        rows, rem = divmod(w.shape[0], n_grid)
        assert rem == 0 and rows % BF16_SUBLANE_TILE == 0, w.shape
        cast_specs.append(pl.BlockSpec((rows, w.shape[1]), grid_pos))
    kv_args, kv_specs, tiles = [], [], []
    for k, vt in sources:
        lk = k.shape[2]
        rows = min(tk, lk)
        tiles.append((rows, lk // rows))
        kv_args += [k, vt]
        kv_specs += [pl.BlockSpec((None, None, lk, HEAD_DIM), lambda bi, h, i: (bi, h, 0, 0)),
                     pl.BlockSpec((None, None, VT_ROWS, lk), lambda bi, h, i: (bi, h, 0, 0))]
    out = pl.pallas_call(
        functools.partial(_flash_kernel, tiles=tuple(tiles), n_cast=len(cast)),
        grid=(b, N_KV_HEADS, nq),
        in_specs=([pl.BlockSpec((gw, tq), lambda bi, h, i: (h, bi * nq + i))] + kv_specs
                  + cast_specs),
        out_specs=[pl.BlockSpec((tq, gw), lambda bi, h, i: (bi * nq + i, h))] + cast_specs,
        out_shape=([jax.ShapeDtypeStruct((q_t.shape[1], q_t.shape[0]), BF16)]
                   + [jax.ShapeDtypeStruct(w.shape, BF16) for w in cast]),
        scratch_shapes=[pltpu.VMEM((GROUP, 1, tq), F32), pltpu.VMEM((GROUP, VT_ROWS, tq), F32),
                        pltpu.VMEM((1, tq), F32)],
        compiler_params=pltpu.CompilerParams(
            dimension_semantics=("parallel", "parallel", "arbitrary"),
            vmem_limit_bytes=VMEM_LIMIT_BYTES),
        name="flash_attn",
    )(q_t, *kv_args, *cast)
    return out[0], out[1:]


MLP_SUB_ROWS = 512


def _mlp_kernel(*refs, with_proj, with_final_norm, ff_chunk):
    refs = list(refs)
    x_ref = refs.pop(0)
    if with_proj:
        o_ref, wo_ref, gate_mix_ref = refs.pop(0), refs.pop(0), refs.pop(0)
    g_ref, shift_ref, scale_ref, gate_ref, w1_ref, w2_ref = refs[:6]
    refs = refs[6:]
    if with_final_norm:
        fg_ref = refs.pop(0)
    (out_ref,) = refs

    sub = min(MLP_SUB_ROWS, x_ref.shape[0])
    for r0 in range(0, x_ref.shape[0], sub):
        rows = slice(r0, r0 + sub)
        x = x_ref[rows, :]
        if with_proj:
            x = x + gate_mix_ref[...] * jnp.dot(o_ref[rows, :], wo_ref[...],
                                                preferred_element_type=F32)
        h = _rms_mod(x, g_ref[...], shift_ref[...], scale_ref[...]).astype(BF16)
        y = jnp.zeros_like(x)
        for c in range(D_FF // ff_chunk):
            sl = slice(c * ff_chunk, (c + 1) * ff_chunk)
            a = jnp.maximum(jnp.dot(h, w1_ref[:, sl], preferred_element_type=F32), 0.0)
            y = y + jnp.dot((a * a).astype(BF16), w2_ref[sl, :], preferred_element_type=F32)
        x = x + gate_ref[...] * y
        if with_final_norm:
            x = x * lax.rsqrt(jnp.mean(x * x, axis=-1, keepdims=True) + EPS) * fg_ref[...]
        out_ref[rows, :] = x


def _mlp(x2d, tm, mod, mod_tiles_per_row, norm_g, w1, w2, layer, *, attn_o=None, wo=None,
         final_g=None):
    t = x2d.shape[0]
    const = lambda i: (0, 0)
    this_layer = lambda i: (layer, 0, 0)
    tile = pl.BlockSpec((tm, D_MODEL), lambda i: (i, 0))
    args, specs = [x2d], [tile]
    if attn_o is not None:
        args += [attn_o, wo, mod]
        specs += [pl.BlockSpec((tm, Q_DIM), lambda i: (i, 0)), _resident((Q_DIM, D_MODEL), const),
                  _mod_spec(2, mod_tiles_per_row)]
    args += [norm_g, mod, mod, mod, w1, w2]
    specs += [pl.BlockSpec((1, D_MODEL), const), _mod_spec(3, mod_tiles_per_row),
              _mod_spec(4, mod_tiles_per_row), _mod_spec(5, mod_tiles_per_row),
              _resident((None, D_MODEL, D_FF), this_layer),
              _resident((None, D_FF, D_MODEL), this_layer)]
    if final_g is not None:
        args.append(final_g)
        specs.append(pl.BlockSpec((1, D_MODEL), const))
    return pl.pallas_call(
        functools.partial(_mlp_kernel, with_proj=attn_o is not None,
                          with_final_norm=final_g is not None, ff_chunk=1024),
        grid=(t // tm,),
        in_specs=specs,
        out_specs=tile,
        out_shape=jax.ShapeDtypeStruct(x2d.shape, F32),
        compiler_params=_params(1),
        name="mlp",
    )(*args)


def _gelu(z):
    return 0.5 * z * (1.0 + lax.erf(z * float(np.sqrt(0.5))))


SGU_COL_BLOCK = 2 * SGU_GROUP_DIM


def _sgu_kernel(x_ref, g_ref, shift_ref, scale_ref, gate_ref, win_ref, bin_ref, vg_ref, ws_ref,
                bs_ref, wout_ref, out_ref, v_scr, t_scr):
    tm = x_ref.shape[0]
    x = x_ref[...]
    h = _rms_mod(x, g_ref[...], shift_ref[...], scale_ref[...]).astype(BF16)
    zv = _gelu(jnp.dot(h, win_ref[:, SGU_DIM:], preferred_element_type=F32) + bin_ref[:, SGU_DIM:])
    r = lax.rsqrt(jnp.mean(zv * zv, axis=-1, keepdims=True) + EPS)
    v_scr[...] = (zv * r * vg_ref[...]).astype(BF16)
    y = jnp.zeros_like(x)
    for cb in range(SGU_DIM // SGU_COL_BLOCK):
        cols = slice(cb * SGU_COL_BLOCK, (cb + 1) * SGU_COL_BLOCK)
        u = _gelu(jnp.dot(h, win_ref[:, cols], preferred_element_type=F32) + bin_ref[:, cols])
        for gi in range(SGU_COL_BLOCK // SGU_GROUP_DIM):
            grp = cb * (SGU_COL_BLOCK // SGU_GROUP_DIM) + gi
            w_s = ws_ref[grp]
            b_s = bs_ref[grp]
            for c in range(tm // CHUNK):
                rows = slice(c * CHUNK, (c + 1) * CHUNK)
                gcols = slice(grp * SGU_GROUP_DIM, (grp + 1) * SGU_GROUP_DIM)
                sv = jnp.dot(w_s, v_scr[rows, gcols], preferred_element_type=F32) + b_s
                lcols = slice(gi * SGU_GROUP_DIM, (gi + 1) * SGU_GROUP_DIM)
                t_scr[rows, lcols] = (u[rows, lcols] * sv).astype(BF16)
        y = y + jnp.dot(t_scr[...], wout_ref[cols, :], preferred_element_type=F32)
    out_ref[...] = x + gate_ref[...] * y


def _sgu(x2d, tm, mod, mod_tiles_per_row, norm_g, w_in, b_in, v_g, w_s, b_s, w_out):
    t = x2d.shape[0]
    const = lambda i: (0, 0)
    const3 = lambda i: (0, 0, 0)
    tile = pl.BlockSpec((tm, D_MODEL), lambda i: (i, 0))
    return pl.pallas_call(
        _sgu_kernel,
        grid=(t // tm,),
        in_specs=[
            tile,
            pl.BlockSpec((1, D_MODEL), const),
            _mod_spec(0, mod_tiles_per_row),
            _mod_spec(1, mod_tiles_per_row),
            _mod_spec(2, mod_tiles_per_row),
            _resident((D_MODEL, 2 * SGU_DIM), const),
            pl.BlockSpec((1, 2 * SGU_DIM), const),
            pl.BlockSpec((1, SGU_DIM), const),
            pl.BlockSpec((SGU_GROUPS, CHUNK, CHUNK), const3),
            pl.BlockSpec((SGU_GROUPS, CHUNK, 1), const3),
            _resident((SGU_DIM, D_MODEL), const),
        ],
        out_specs=tile,
        out_shape=jax.ShapeDtypeStruct(x2d.shape, F32),
        scratch_shapes=[pltpu.VMEM((tm, SGU_DIM), BF16), pltpu.VMEM((tm, SGU_COL_BLOCK), BF16)],
        compiler_params=_params(1),
        name="sgu",
    )(x2d, norm_g, mod, mod, mod, w_in, b_in, v_g, w_s, b_s, w_out)


def _deinterleave_heads(n_heads):
    half = np.concatenate([np.arange(0, HEAD_DIM, 2), np.arange(1, HEAD_DIM, 2)])
    return (np.arange(n_heads)[:, None] * HEAD_DIM + half[None, :]).reshape(-1)


def kernel(x, c, ctx, c_ctx, ada_w, ada_b, mix_norm_g, mlp_norm_g, mlp_w1, mlp_w2, attn_wqkv,
           attn_q_g, attn_k_g, attn_wo, sgu_w_in, sgu_b_in, sgu_v_g, sgu_w_s, sgu_b_s, sgu_w_out,
           final_g):
    b, n, d = x.shape
    lc = ctx.shape[1]
    tm_x, tm_c, tm_mlp = 512, 256, 1024
    tq, tk = 1024, 1024

    cond = jnp.concatenate([c, c_ctx[None, :], jnp.zeros((MOD_ROWS - b - 1, d), F32)], axis=0)
    mods = _adaln(cond, ada_w, ada_b)
    cos_t, sin_t = _rope_tables(n)
    cos_c = jnp.ones((HEAD_DIM // 2, lc), F32)
    sin_c = jnp.zeros((HEAD_DIM // 2, lc), F32)
    perm = np.concatenate([_deinterleave_heads(N_HEADS),
                           Q_DIM + _deinterleave_heads(N_KV_HEADS),
                           np.arange(Q_DIM + KV_DIM, QKV_DIM)])
    head_perm = _deinterleave_heads(1)

    x2 = x.reshape(b * n, d)
    c2 = ctx.reshape(b * lc, d)
    row = lambda v: v.reshape(1, -1)
    stacked = dict(w1=mlp_w1, w2=mlp_w2, wo=attn_wo, w_in=sgu_w_in, w_out=sgu_w_out)
    bf16 = None
    for i in range(DEPTH):
        last = i == DEPTH - 1
        use_attn = (i % N_MIXERS) == 0
        j = i // N_MIXERS
        mod_x = mods[i, :b].reshape(b, 1, N_MOD * d)
        mod_c = mods[i, b:b + 1].reshape(1, 1, N_MOD * d)
        x_tpr, c_tpr = n // tm_x, (b * lc) // tm_c
        mix_g = row(mix_norm_g[i])
        mlp_g = row(mlp_norm_g[i])
        fg = row(final_g) if last else None
        if use_attn:
            wqkv_t = attn_wqkv[j][:, perm].T.astype(BF16)
            q_g = attn_q_g[j][head_perm][:, None]
            k_g = attn_k_g[j][head_perm][:, None]
            q_t, k, vt = _qkv(x2, n, tm_x, mod_x, x_tpr, mix_g, wqkv_t, q_g, k_g, cos_t, sin_t)
            qc_t, kc, vtc = _qkv(c2, lc, tm_c, mod_c, c_tpr, mix_g, wqkv_t, q_g, k_g,
                                 cos_c, sin_c)
            to_cast = () if bf16 else tuple(w.reshape(-1, w.shape[-1]) for w in stacked.values())
            o, casted = _flash(q_t, [(k, vt), (kc, vtc)], n, tq, tk, cast=to_cast)
            if not bf16:
                bf16 = {name: wb.reshape(w.shape)
                        for (name, w), wb in zip(stacked.items(), casted)}
            w1, w2, wo = bf16["w1"], bf16["w2"], bf16["wo"][j]
            x2 = _mlp(x2, tm_mlp, mod_x, n // tm_mlp, mlp_g, w1, w2, i, attn_o=o, wo=wo,
                      final_g=fg)
            if not last:
                oc, _ = _flash(qc_t, [(kc, vtc)], lc, lc, tk)
                c2 = _mlp(c2, tm_c, mod_c, c_tpr, mlp_g, w1, w2, i, attn_o=oc, wo=wo)
        else:
            w1, w2 = bf16["w1"], bf16["w2"]
            sgu_args = (bf16["w_in"][j], row(sgu_b_in[j]), row(sgu_v_g[j]),
                        sgu_w_s[j].astype(BF16), sgu_b_s[j][:, :, None], bf16["w_out"][j])
            x2 = _sgu(x2, 4 * CHUNK, mod_x, n // (4 * CHUNK), mix_g, *sgu_args)
            x2 = _mlp(x2, tm_mlp, mod_x, n // tm_mlp, mlp_g, w1, w2, i, final_g=fg)
            if not last:
                c2 = _sgu(c2, tm_c, mod_c, c_tpr, mix_g, *sgu_args)
                c2 = _mlp(c2, tm_c, mod_c, c_tpr, mlp_g, w1, w2, i)
    return x2.reshape(b, n, d)
```

```python
import functools

import jax
import jax.numpy as jnp
import numpy as np
from jax import lax
from jax.experimental import pallas as pl
from jax.experimental.pallas import tpu as pltpu

D_MODEL = 1024
DEPTH = 2
GRID_W = 64
N_MIXERS = 2
N_HEADS = 8
N_KV_HEADS = 2
HEAD_DIM = 128
GROUP = N_HEADS // N_KV_HEADS
Q_DIM = N_HEADS * HEAD_DIM
KV_DIM = N_KV_HEADS * HEAD_DIM
QKV_DIM = Q_DIM + 2 * KV_DIM
ROPE_AXIS_DIM = HEAD_DIM // 2
ROPE_THETA = 10000.0
CHUNK = 128
SGU_DIM = 3 * D_MODEL
SGU_GROUPS = 8
SGU_GROUP_DIM = SGU_DIM // SGU_GROUPS
D_FF = 4 * D_MODEL
N_MOD = 6
EPS = 1e-6

F32 = jnp.float32
BF16 = jnp.bfloat16

VMEM_LIMIT_BYTES = 56 * 1024 * 1024
MOD_ROWS = 8

QKV_TILE = 512
MLP_TILE = 1024
SGU_TILE = 4 * CHUNK
FLASH_Q_TILE = 1024
FLASH_K_TILE = 1024

SOFTMAX_Q_SCALE = float(HEAD_DIM ** -0.5 * np.log2(np.e))
NEG_BIG = -1e30
BF16_SUBLANE_TILE = 16
VT_ROWS = HEAD_DIM + BF16_SUBLANE_TILE


def _params(n_axes):
    return pltpu.CompilerParams(dimension_semantics=("parallel",) * n_axes,
                                vmem_limit_bytes=VMEM_LIMIT_BYTES)


def _resident(shape, index_map):
    return pl.BlockSpec(shape, index_map, pipeline_mode=pl.Buffered(1))


def _rms_mod(x, g, shift, scale):
    y = x * lax.rsqrt(jnp.mean(x * x, axis=-1, keepdims=True) + EPS) * g
    return y * (1.0 + scale) + shift


def _mod_spec(k, tiles_per_row):
    return pl.BlockSpec((None, 1, D_MODEL), lambda i: (i // tiles_per_row, 0, k))


def _adaln_kernel(c_ref, w_ref, b_ref, o_ref):
    c = c_ref[...]
    s = c * jax.nn.sigmoid(c)
    o_ref[...] = jnp.dot(s.astype(BF16), w_ref[...].astype(BF16),
                         preferred_element_type=F32) + b_ref[...]


def _adaln(cond, ada_w, ada_b):
    depth = ada_w.shape[0]
    return pl.pallas_call(
        _adaln_kernel,
        grid=(depth, N_MOD),
        in_specs=[
            pl.BlockSpec((MOD_ROWS, D_MODEL), lambda i, j: (0, 0)),
            pl.BlockSpec((None, D_MODEL, D_MODEL), lambda i, j: (i, 0, j)),
            pl.BlockSpec((None, 1, D_MODEL), lambda i, j: (i, 0, j)),
        ],
        out_specs=pl.BlockSpec((None, MOD_ROWS, D_MODEL), lambda i, j: (i, 0, j)),
        out_shape=jax.ShapeDtypeStruct((depth, MOD_ROWS, N_MOD * D_MODEL), F32),
        compiler_params=_params(2),
        name="adaln",
    )(cond, ada_w, ada_b.reshape(depth, 1, N_MOD * D_MODEL))


def _cos_sin_kernel(ang_ref, cos_ref, sin_ref):
    a = ang_ref[...]
    cos_ref[...] = jnp.cos(a)
    sin_ref[...] = jnp.sin(a)


def _rope_tables(n):
    rows_count = n // GRID_W
    freqs = 1.0 / (ROPE_THETA ** (jnp.arange(0, ROPE_AXIS_DIM, 2, dtype=F32) / ROPE_AXIS_DIM))
    pos = jnp.arange(rows_count + GRID_W, dtype=jnp.int32)
    pos = jnp.where(pos < rows_count, pos, pos - rows_count).astype(F32)
    ang = pos[:, None] * freqs[None, :]
    cos_s, sin_s = pl.pallas_call(
        _cos_sin_kernel,
        out_shape=(jax.ShapeDtypeStruct(ang.shape, F32),) * 2,
        name="rope_cos_sin",
    )(ang)

    def expand(t):
        by_row = jnp.repeat(t[:rows_count], GRID_W, axis=0)
        by_col = jnp.tile(t[rows_count:], (rows_count, 1))
        return jnp.concatenate([by_row, by_col], axis=-1)

    return expand(cos_s).T, expand(sin_s).T


Q_HEADS_PER_DOT = 4


def _qkv_kernel(x_ref, g_ref, shift_ref, scale_ref, wt_ref, qg_ref, kg_ref, cos_ref, sin_ref,
                qt_ref, k_ref, vt_ref):
    h =_rms_mod(x_ref[...], g_ref[...], shift_ref[...], scale_ref[...]).astype(BF16)
    cos = cos_ref[...]
    sin = sin_ref[...]
    half = HEAD_DIM // 2
    nt_dims = (((1,), (1,)), ((), ()))

    def norm_rope(t, g):
        t = t * lax.rsqrt(jnp.mean(t * t, axis=0, keepdims=True) + EPS) * g
        x1, x2 = t[:half], t[half:]
        return x1 * cos - x2 * sin, x1 * sin + x2 * cos

    qg = qg_ref[...]
    for blk in range(N_HEADS // Q_HEADS_PER_DOT):
        rows = slice(blk * Q_HEADS_PER_DOT * HEAD_DIM, (blk + 1) * Q_HEADS_PER_DOT * HEAD_DIM)
        yt = lax.dot_general(wt_ref[rows, :], h, nt_dims, preferred_element_type=F32)
        for j in range(Q_HEADS_PER_DOT):
            r0 = (blk * Q_HEADS_PER_DOT + j) * HEAD_DIM
            o1, o2 = norm_rope(yt[j * HEAD_DIM:(j + 1) * HEAD_DIM], qg)
            qt_ref[r0:r0 + half, :] = (o1 * SOFTMAX_Q_SCALE).astype(BF16)
            qt_ref[r0 + half:r0 + HEAD_DIM, :] = (o2 * SOFTMAX_Q_SCALE).astype(BF16)
    yt = lax.dot_general(wt_ref[Q_DIM:, :], h, nt_dims, preferred_element_type=F32)
    kg = kg_ref[...]
    for hd in range(N_KV_HEADS):
        o1, o2 = norm_rope(yt[hd * HEAD_DIM:(hd + 1) * HEAD_DIM], kg)
        k_ref[hd] = jnp.concatenate([o1, o2], axis=0).T.astype(BF16)
        v_t = yt[KV_DIM + hd * HEAD_DIM:KV_DIM + (hd + 1) * HEAD_DIM]
        vt_ref[hd, :HEAD_DIM, :] = v_t.astype(BF16)
        vt_ref[hd, HEAD_DIM:, :] = jnp.ones((VT_ROWS - HEAD_DIM, v_t.shape[1]), BF16)


def _qkv(x2d, seq, tm, mod, mod_tiles_per_row, norm_g, wqkv_t, q_g, k_g, cos_t, sin_t):
    t = x2d.shape[0]
    tpb = seq // tm
    const = lambda i: (0, 0)
    return pl.pallas_call(
        _qkv_kernel,
        grid=(t // tm,),
        in_specs=[
            pl.BlockSpec((tm, D_MODEL), lambda i: (i, 0)),
            pl.BlockSpec((1, D_MODEL), const),
            _mod_spec(0, mod_tiles_per_row),
            _mod_spec(1, mod_tiles_per_row),
            _resident((QKV_DIM, D_MODEL), const),
            pl.BlockSpec((HEAD_DIM, 1), const),
            pl.BlockSpec((HEAD_DIM, 1), const),
            pl.BlockSpec((HEAD_DIM // 2, tm), lambda i: (0, i % tpb)),
            pl.BlockSpec((HEAD_DIM // 2, tm), lambda i: (0, i % tpb)),
        ],
        out_specs=[
            pl.BlockSpec((Q_DIM, tm), lambda i: (0, i)),
            pl.BlockSpec((None, N_KV_HEADS, tm, HEAD_DIM), lambda i: (i // tpb, 0, i % tpb, 0)),
            pl.BlockSpec((None, N_KV_HEADS, VT_ROWS, tm), lambda i: (i // tpb, 0, 0, i % tpb)),
        ],
        out_shape=[
            jax.ShapeDtypeStruct((Q_DIM, t), BF16),
            jax.ShapeDtypeStruct((t // seq, N_KV_HEADS, seq, HEAD_DIM), BF16),
            jax.ShapeDtypeStruct((t // seq, N_KV_HEADS, VT_ROWS, seq), BF16),
        ],
        compiler_params=_params(1),
        name="qkv_proj",
    )(x2d, norm_g, mod, mod, wqkv_t, q_g, k_g, cos_t, sin_t)


MAX_TILES_PER_HEAD = 16
L_MIN = 2.0 ** -60


def _flash_kernel(*refs, tiles, n_cast):
    n_src = len(tiles)
    n_in = 1 + 2 * n_src + n_cast
    q_ref = refs[0]
    k_refs = refs[1:1 + 2 * n_src:2]
    vt_refs = refs[2:2 + 2 * n_src:2]
    o_ref = refs[n_in]
    shift_scr, acc_scr, kmax_scr = refs[n_in + 1 + n_cast:]

    for w_ref, w_bf16_ref in zip(refs[n_in - n_cast:n_in], refs[n_in + 1:n_in + 1 + n_cast]):
        w_bf16_ref[...] = w_ref[...].astype(BF16)

    @pl.when(pl.program_id(2) == 0)
    def _():
        kmax2 = jnp.zeros((1, 1), F32)
        for k_ref, (tk, nk) in zip(k_refs, tiles):
            def body(t, n2, k_ref=k_ref, tk=tk):
                kk = k_ref[pl.ds(pl.multiple_of(t * tk, tk), tk), :].astype(F32)
                return jnp.maximum(n2, jnp.sum(kk * kk, axis=1, keepdims=True))

            n2 = lax.fori_loop(0, nk, body, jnp.zeros((tk, 1), F32))
            kmax2 = jnp.maximum(kmax2, jnp.max(n2, axis=0, keepdims=True))
        kmax_scr[...] = jnp.broadcast_to(jnp.sqrt(kmax2), kmax_scr.shape)

    def operands(src, g, t):
        tk = tiles[src][0]
        ks = t * tk if isinstance(t, int) else pl.multiple_of(t * tk, tk)
        qs = g * HEAD_DIM if isinstance(g, int) else pl.multiple_of(g * HEAD_DIM, HEAD_DIM)
        q_t = q_ref[pl.ds(qs, HEAD_DIM), :]
        return k_refs[src][pl.ds(ks, tk), :], q_t, vt_refs[src][:, pl.ds(ks, tk)]

    def finalize():
        l_min = None
        for g in range(GROUP):
            acc = acc_scr[g]
            l = acc[HEAD_DIM:HEAD_DIM + 1]
            o = acc[:HEAD_DIM] * (1.0 / l)
            o_ref[:, g * HEAD_DIM:(g + 1) * HEAD_DIM] = o.astype(BF16).T
            l_min = l if l_min is None else jnp.minimum(l_min, l)
        return jnp.min(l_min)

    for g in range(GROUP):
        qf = q_ref[g * HEAD_DIM:(g + 1) * HEAD_DIM, :].astype(F32)
        shift_scr[g] = jnp.sqrt(jnp.sum(qf * qf, axis=0, keepdims=True)) * kmax_scr[...]

    def bounded_step(src, g, t):
        k_tile, q_t, vt_tile = operands(src, g, t)
        s = jnp.dot(k_tile, q_t, preferred_element_type=F32)
        p = jnp.exp2(s - shift_scr[g]).astype(BF16)
        pv = jnp.dot(vt_tile, p, preferred_element_type=F32)
        if (src, t) == (0, 0):
            acc_scr[g] = pv
        else:
            acc_scr[g] += pv

    def per_head(step_fn):
        def body(g, carry):
            for src, (_, nk) in enumerate(tiles):
                for t in range(nk):
                    step_fn(src, g, t)
            return carry

        lax.fori_loop(0, GROUP, body, 0)

    per_head(bounded_step)
    l_min = finalize()

    @pl.when(jnp.logical_not(l_min >= L_MIN))
    def _():
        shift_scr[...] = jnp.full(shift_scr.shape, NEG_BIG, F32)
        acc_scr[...] = jnp.zeros(acc_scr.shape, F32)

        def online_step(src, g, t):
            k_tile, q_t, vt_tile = operands(src, g, t)
            s = jnp.dot(k_tile, q_t, preferred_element_type=F32)
            m_old = shift_scr[g]
            m_new = jnp.maximum(m_old, jnp.max(s, axis=0, keepdims=True))
            shift_scr[g] = m_new
            p = jnp.exp2(s - m_new).astype(BF16)
            pv = jnp.dot(vt_tile, p, preferred_element_type=F32)
            acc_scr[g] = jnp.exp2(m_old - m_new) * acc_scr[g] + pv

        for src, (_, nk) in enumerate(tiles):
            def body(u, carry, src=src, nk=nk):
                g = u // nk
                online_step(src, g, u - g * nk)
                return carry

            lax.fori_loop(0, GROUP * nk, body, 0)
        finalize()


def _flash(q_t, sources, seq_q, tq, tk, cast=()):
    b = sources[0][0].shape[0]
    nq = seq_q // tq
    gw = GROUP * HEAD_DIM
    n_grid = b * N_KV_HEADS * nq
    grid_pos = lambda bi, h, i: ((bi * N_KV_HEADS + h) * nq + i, 0)
    cast_specs = []
    for w in cast:
        rows, rem = divmod(w.shape[0], n_grid)
        assert rem == 0 and rows % BF16_SUBLANE_TILE == 0, w.shape
        cast_specs.append(pl.BlockSpec((rows, w.shape[1]), grid_pos))
    kv_args, kv_specs, tiles = [], [], []
    for k, vt in sources:
        lk = k.shape[2]
        rows = min(tk, lk)
        tiles.append((rows, lk // rows))
        kv_args += [k, vt]
        assert sum(nk for _, nk in tiles) <= MAX_TILES_PER_HEAD, tiles
        kv_specs += [pl.BlockSpec((None, None, lk, HEAD_DIM), lambda bi, h, i: (bi, h, 0, 0)),
                     pl.BlockSpec((None, None, VT_ROWS, lk), lambda bi, h, i: (bi, h, 0, 0))]
    out = pl.pallas_call(
        functools.partial(_flash_kernel, tiles=tuple(tiles), n_cast=len(cast)),
        grid=(b, N_KV_HEADS, nq),
        in_specs=([pl.BlockSpec((gw, tq), lambda bi, h, i: (h, bi * nq + i))] + kv_specs
                  + cast_specs),
        out_specs=[pl.BlockSpec((tq, gw), lambda bi, h, i: (bi * nq + i, h))] + cast_specs,
        out_shape=([jax.ShapeDtypeStruct((q_t.shape[1], q_t.shape[0]), BF16)]
                   + [jax.ShapeDtypeStruct(w.shape, BF16) for w in cast]),
        scratch_shapes=[pltpu.VMEM((GROUP, 1, tq), F32), pltpu.VMEM((GROUP, VT_ROWS, tq), F32),
                        pltpu.VMEM((1, tq), F32)],
        compiler_params=pltpu.CompilerParams(
            dimension_semantics=("parallel", "parallel", "arbitrary"),
            vmem_limit_bytes=VMEM_LIMIT_BYTES),
        name="flash_attn",
    )(q_t, *kv_args, *cast)
    return out[0], out[1:]


MLP_SUB_ROWS = 512


def _mlp_kernel(*refs, with_proj, with_final_norm, ff_chunk):
    refs = list(refs)
    x_ref = refs.pop(0)
    if with_proj:
        o_ref, wo_ref, gate_mix_ref = refs.pop(0), refs.pop(0), refs.pop(0)
    g_ref, shift_ref, scale_ref, gate_ref, w1_ref, w2_ref = refs[:6]
    refs = refs[6:]
    if with_final_norm:
        fg_ref = refs.pop(0)
    (out_ref,) = refs

    sub = min(MLP_SUB_ROWS, x_ref.shape[0])
    for r0 in range(0, x_ref.shape[0], sub):
        rows = slice(r0, r0 + sub)
        x = x_ref[rows, :]
        if with_proj:
            x = x + gate_mix_ref[...] * jnp.dot(o_ref[rows, :], wo_ref[...],
                                                preferred_element_type=F32)
        h = _rms_mod(x, g_ref[...], shift_ref[...], scale_ref[...]).astype(BF16)
        y = jnp.zeros_like(x)
        for c in range(D_FF // ff_chunk):
            sl = slice(c * ff_chunk, (c + 1) * ff_chunk)
            a = jnp.maximum(jnp.dot(h, w1_ref[:, sl], preferred_element_type=F32), 0.0)
            y = y + jnp.dot((a * a).astype(BF16), w2_ref[sl, :], preferred_element_type=F32)
        x = x + gate_ref[...] * y
        if with_final_norm:
            x = x * lax.rsqrt(jnp.mean(x * x, axis=-1, keepdims=True) + EPS) * fg_ref[...]
        out_ref[rows, :] = x


def _mlp(x2d, tm, mod, mod_tiles_per_row, norm_g, w1, w2, layer, *, attn_o=None, wo=None,
         final_g=None):
    t = x2d.shape[0]
    const = lambda i: (0, 0)
    this_layer = lambda i: (layer, 0, 0)
    tile = pl.BlockSpec((tm, D_MODEL), lambda i: (i, 0))
    args, specs = [x2d], [tile]
    if attn_o is not None:
        args += [attn_o, wo, mod]
        specs += [pl.BlockSpec((tm, Q_DIM), lambda i: (i, 0)), _resident((Q_DIM, D_MODEL), const),
                  _mod_spec(2, mod_tiles_per_row)]
    args += [norm_g, mod, mod, mod, w1, w2]
    specs += [pl.BlockSpec((1, D_MODEL), const), _mod_spec(3, mod_tiles_per_row),
              _mod_spec(4, mod_tiles_per_row), _mod_spec(5, mod_tiles_per_row),
              _resident((None, D_MODEL, D_FF), this_layer),
              _resident((None, D_FF, D_MODEL), this_layer)]
    if final_g is not None:
        args.append(final_g)
        specs.append(pl.BlockSpec((1, D_MODEL), const))
    return pl.pallas_call(
        functools.partial(_mlp_kernel, with_proj=attn_o is not None,
                          with_final_norm=final_g is not None, ff_chunk=1024),
        grid=(t // tm,),
        in_specs=specs,
        out_specs=tile,
        out_shape=jax.ShapeDtypeStruct(x2d.shape, F32),
        compiler_params=_params(1),
        name="mlp",
    )(*args)


def _gelu(z):
    return 0.5 * z * (1.0 + lax.erf(z * float(np.sqrt(0.5))))


SGU_COL_BLOCK = 2 * SGU_GROUP_DIM


def _sgu_kernel(x_ref, g_ref, shift_ref, scale_ref, gate_ref, win_ref, bin_ref, vg_ref, ws_ref,
                bs_ref, wout_ref, out_ref, v_scr, t_scr):
    tm = x_ref.shape[0]
    x = x_ref[...]
    h = _rms_mod(x, g_ref[...], shift_ref[...], scale_ref[...]).astype(BF16)
    zv = _gelu(jnp.dot(h, win_ref[:, SGU_DIM:], preferred_element_type=F32) + bin_ref[:, SGU_DIM:])
    r = lax.rsqrt(jnp.mean(zv * zv, axis=-1, keepdims=True) + EPS)
    v_scr[...] = (zv * r * vg_ref[...]).astype(BF16)
    for cb in range(SGU_DIM // SGU_COL_BLOCK):
        cols = slice(cb * SGU_COL_BLOCK, (cb + 1) * SGU_COL_BLOCK)
        u = _gelu(jnp.dot(h, win_ref[:, cols], preferred_element_type=F32) + bin_ref[:, cols])
        for gi in range(SGU_COL_BLOCK // SGU_GROUP_DIM):
            grp = cb * (SGU_COL_BLOCK // SGU_GROUP_DIM) + gi
            w_s = ws_ref[grp]
            b_s = bs_ref[grp]
            for c in range(tm // CHUNK):
                rows = slice(c * CHUNK, (c + 1) * CHUNK)
                gcols = slice(grp * SGU_GROUP_DIM, (grp + 1) * SGU_GROUP_DIM)
                sv = jnp.dot(w_s, v_scr[rows, gcols], preferred_element_type=F32) + b_s
                lcols = slice(gi * SGU_GROUP_DIM, (gi + 1) * SGU_GROUP_DIM)
                t_scr[rows, gcols] = (u[rows, lcols] * sv).astype(BF16)
    y = jnp.dot(t_scr[...], wout_ref[...], preferred_element_type=F32)
    out_ref[...] = x + gate_ref[...] * y


def _sgu(x2d, tm, mod, mod_tiles_per_row, norm_g, w_in, b_in, v_g, w_s, b_s, w_out):
    t = x2d.shape[0]
    const = lambda i: (0, 0)
    const3 = lambda i: (0, 0, 0)
    tile = pl.BlockSpec((tm, D_MODEL), lambda i: (i, 0))
    return pl.pallas_call(
        _sgu_kernel,
        grid=(t // tm,),
        in_specs=[
            tile,
            pl.BlockSpec((1, D_MODEL), const),
            _mod_spec(0, mod_tiles_per_row),
            _mod_spec(1, mod_tiles_per_row),
            _mod_spec(2, mod_tiles_per_row),
            _resident((D_MODEL, 2 * SGU_DIM), const),
            pl.BlockSpec((1, 2 * SGU_DIM), const),
            pl.BlockSpec((1, SGU_DIM), const),
            pl.BlockSpec((SGU_GROUPS, CHUNK, CHUNK), const3),
            pl.BlockSpec((SGU_GROUPS, CHUNK, 1), const3),
            _resident((SGU_DIM, D_MODEL), const),
        ],
        out_specs=tile,
        out_shape=jax.ShapeDtypeStruct(x2d.shape, F32),
        scratch_shapes=[pltpu.VMEM((tm, SGU_DIM), BF16), pltpu.VMEM((tm, SGU_DIM), BF16)],
        compiler_params=_params(1),
        name="sgu",
    )(x2d, norm_g, mod, mod, mod, w_in, b_in, v_g, w_s, b_s, w_out)


def _pairs_apart(w, n_heads):
    split = w.reshape(n_heads, HEAD_DIM // 2, 2, *w.shape[1:])
    return jnp.swapaxes(split, 1, 2).reshape(w.shape)


def kernel(x, c, ctx, c_ctx, ada_w, ada_b, mix_norm_g, mlp_norm_g, mlp_w1, mlp_w2, attn_wqkv,
           attn_q_g, attn_k_g, attn_wo, sgu_w_in, sgu_b_in, sgu_v_g, sgu_w_s, sgu_b_s, sgu_w_out,
           final_g):
    b, n, d = x.shape
    lc = ctx.shape[1]
    tm_x, tm_mlp, tm_sgu, tq, tk = QKV_TILE, MLP_TILE, SGU_TILE, FLASH_Q_TILE, FLASH_K_TILE
    tm_c = lc

    cond = jnp.concatenate([c, c_ctx[None, :], jnp.zeros((MOD_ROWS - b - 1, d), F32)], axis=0)
    mods = _adaln(cond, ada_w, ada_b)
    cos_t, sin_t = _rope_tables(n)
    cos_c = jnp.ones((HEAD_DIM // 2, lc), F32)
    sin_c = jnp.zeros((HEAD_DIM // 2, lc), F32)

    x2 = x.reshape(b * n, d)
    c2 = ctx.reshape(b * lc, d)
    row = lambda v: v.reshape(1, -1)
    stacked = dict(w1=mlp_w1, w2=mlp_w2, wo=attn_wo, w_in=sgu_w_in, w_out=sgu_w_out)
    bf16 = None
    for i in range(DEPTH):
        last = i == DEPTH - 1
        use_attn = (i % N_MIXERS) == 0
        j = i // N_MIXERS
        mod_x = mods[i, :b].reshape(b, 1, N_MOD * d)
        mod_c = mods[i, b:b + 1].reshape(1, 1, N_MOD * d)
        x_tpr, c_tpr = n // tm_x, (b * lc) // tm_c
        mix_g = row(mix_norm_g[i])
        mlp_g = row(mlp_norm_g[i])
        fg = row(final_g) if last else None
        if use_attn:
            w_t = attn_wqkv[j].T
            wqkv_t = jnp.concatenate(
                [_pairs_apart(w_t[:Q_DIM + KV_DIM], N_HEADS + N_KV_HEADS), w_t[Q_DIM + KV_DIM:]],
                axis=0).astype(BF16)
            q_g = _pairs_apart(attn_q_g[j], 1)[:, None]
            k_g = _pairs_apart(attn_k_g[j], 1)[:, None]
            q_t, k, vt = _qkv(x2, n, tm_x, mod_x, x_tpr, mix_g, wqkv_t, q_g, k_g, cos_t, sin_t)
            qc_t, kc, vtc = _qkv(c2, lc, tm_c, mod_c, c_tpr, mix_g, wqkv_t, q_g, k_g,
                                 cos_c, sin_c)
            to_cast = () if bf16 else tuple(w.reshape(-1, w.shape[-1]) for w in stacked.values())
            o, casted = _flash(q_t, [(k, vt), (kc, vtc)], n, tq, tk, cast=to_cast)
            if not bf16:
                bf16 = {name: wb.reshape(w.shape)
                        for (name, w), wb in zip(stacked.items(), casted)}
            w1, w2, wo = bf16["w1"], bf16["w2"], bf16["wo"][j]
            x2 = _mlp(x2, tm_mlp, mod_x, n // tm_mlp, mlp_g, w1, w2, i, attn_o=o, wo=wo,
                      final_g=fg)
            if not last:
                oc, _ = _flash(qc_t, [(kc, vtc)], lc, lc, tk)
                c2 = _mlp(c2, tm_c, mod_c, c_tpr, mlp_g, w1, w2, i, attn_o=oc, wo=wo)
        else:
            w1, w2 = bf16["w1"], bf16["w2"]
            sgu_args = (bf16["w_in"][j], row(sgu_b_in[j]), row(sgu_v_g[j]),
                        sgu_w_s[j].astype(BF16), sgu_b_s[j][:, :, None], bf16["w_out"][j])
            x2 = _sgu(x2, tm_sgu, mod_x, n // tm_sgu, mix_g, *sgu_args)
            x2 = _mlp(x2, tm_mlp, mod_x, n // tm_mlp, mlp_g, w1, w2, i, final_g=fg)
            if not last:
                c2 = _sgu(c2, tm_c, mod_c, c_tpr, mix_g, *sgu_args)
                c2 = _mlp(c2, tm_c, mod_c, c_tpr, mlp_g, w1, w2, i)
    return x2.reshape(b, n, d)
```

```python
import functools

import jax
import jax.numpy as jnp
import numpy as np
from jax import lax
from jax.experimental import pallas as pl
from jax.experimental.pallas import tpu as pltpu

D_MODEL = 1024
DEPTH = 2
GRID_W = 64
N_MIXERS = 2
N_HEADS = 8
N_KV_HEADS = 2
HEAD_DIM = 128
GROUP = N_HEADS // N_KV_HEADS
Q_DIM = N_HEADS * HEAD_DIM
KV_DIM = N_KV_HEADS * HEAD_DIM
QKV_DIM = Q_DIM + 2 * KV_DIM
ROPE_AXIS_DIM = HEAD_DIM // 2
ROPE_THETA = 10000.0
CHUNK = 128
SGU_DIM = 3 * D_MODEL
SGU_GROUPS = 8
SGU_GROUP_DIM = SGU_DIM // SGU_GROUPS
D_FF = 4 * D_MODEL
N_MOD = 6
EPS = 1e-6

F32 = jnp.float32
BF16 = jnp.bfloat16

VMEM_LIMIT_BYTES = 56 * 1024 * 1024
MOD_ROWS = 8

QKV_TILE = 512
MLP_TILE = 1024
SGU_TILE = 4 * CHUNK
FLASH_Q_TILE = 1024
FLASH_K_TILE = 1024

SOFTMAX_Q_SCALE = float(HEAD_DIM ** -0.5 * np.log2(np.e))
NEG_BIG = -1e30
BF16_SUBLANE_TILE = 16


def _params(n_axes):
    return pltpu.CompilerParams(dimension_semantics=("parallel",) * n_axes,
                                vmem_limit_bytes=VMEM_LIMIT_BYTES)


def _resident(shape, index_map):
    return pl.BlockSpec(shape, index_map, pipeline_mode=pl.Buffered(1))


def _rms_mod(x, g, shift, scale):
    y = x * lax.rsqrt(jnp.mean(x * x, axis=-1, keepdims=True) + EPS) * g
    return y * (1.0 + scale) + shift


def _mod_spec(k, tiles_per_row):
    return pl.BlockSpec((None, 1, D_MODEL), lambda i: (i // tiles_per_row, 0, k))


def _adaln_kernel(c_ref, w_ref, b_ref, o_ref):
    c = c_ref[...]
    s = c * jax.nn.sigmoid(c)
    o_ref[...] = jnp.dot(s.astype(BF16), w_ref[...].astype(BF16),
                         preferred_element_type=F32) + b_ref[...]


def _adaln(cond, ada_w, ada_b):
    depth = ada_w.shape[0]
    return pl.pallas_call(
        _adaln_kernel,
        grid=(depth, N_MOD),
        in_specs=[
            pl.BlockSpec((MOD_ROWS, D_MODEL), lambda i, j: (0, 0)),
            pl.BlockSpec((None, D_MODEL, D_MODEL), lambda i, j: (i, 0, j)),
            pl.BlockSpec((None, 1, D_MODEL), lambda i, j: (i, 0, j)),
        ],
        out_specs=pl.BlockSpec((None, MOD_ROWS, D_MODEL), lambda i, j: (i, 0, j)),
        out_shape=jax.ShapeDtypeStruct((depth, MOD_ROWS, N_MOD * D_MODEL), F32),
        compiler_params=_params(2),
        name="adaln",
    )(cond, ada_w, ada_b.reshape(depth, 1, N_MOD * D_MODEL))


def _cos_sin_kernel(ang_ref, cos_ref, sin_ref):
    a = ang_ref[...]
    cos_ref[...] = jnp.cos(a)
    sin_ref[...] = jnp.sin(a)


def _rope_tables(n):
    rows_count = n // GRID_W
    freqs = 1.0 / (ROPE_THETA ** (jnp.arange(0, ROPE_AXIS_DIM, 2, dtype=F32) / ROPE_AXIS_DIM))
    pos = jnp.arange(rows_count + GRID_W, dtype=jnp.int32)
    pos = jnp.where(pos < rows_count, pos, pos - rows_count).astype(F32)
    ang = pos[:, None] * freqs[None, :]
    cos_s, sin_s = pl.pallas_call(
        _cos_sin_kernel,
        out_shape=(jax.ShapeDtypeStruct(ang.shape, F32),) * 2,
        name="rope_cos_sin",
    )(ang)

    def expand(t):
        by_row = jnp.repeat(t[:rows_count], GRID_W, axis=0)
        by_col = jnp.tile(t[rows_count:], (rows_count, 1))
        return jnp.concatenate([by_row, by_col], axis=-1)

    return expand(cos_s).T, expand(sin_s).T


Q_HEADS_PER_DOT = 4


def _qkv_kernel(x_ref, g_ref, shift_ref, scale_ref, wt_ref, qg_ref, kg_ref, cos_ref, sin_ref,
                qt_ref, k_ref, vt_ref):
    h =_rms_mod(x_ref[...], g_ref[...], shift_ref[...], scale_ref[...]).astype(BF16)
    cos = cos_ref[...]
    sin = sin_ref[...]
    half = HEAD_DIM // 2
    nt_dims = (((1,), (1,)), ((), ()))

    def norm_rope(t, g):
        t = t * lax.rsqrt(jnp.mean(t * t, axis=0, keepdims=True) + EPS) * g
        x1, x2 = t[:half], t[half:]
        return x1 * cos - x2 * sin, x1 * sin + x2 * cos

    qg = qg_ref[...]
    for blk in range(N_HEADS // Q_HEADS_PER_DOT):
        rows = slice(blk * Q_HEADS_PER_DOT * HEAD_DIM, (blk + 1) * Q_HEADS_PER_DOT * HEAD_DIM)
        yt = lax.dot_general(wt_ref[rows, :], h, nt_dims, preferred_element_type=F32)
        for j in range(Q_HEADS_PER_DOT):
            r0 = (blk * Q_HEADS_PER_DOT + j) * HEAD_DIM
            o1, o2 = norm_rope(yt[j * HEAD_DIM:(j + 1) * HEAD_DIM], qg)
            qt_ref[r0:r0 + half, :] = (o1 * SOFTMAX_Q_SCALE).astype(BF16)
            qt_ref[r0 + half:r0 + HEAD_DIM, :] = (o2 * SOFTMAX_Q_SCALE).astype(BF16)
    yt = lax.dot_general(wt_ref[Q_DIM:, :], h, nt_dims, preferred_element_type=F32)
    kg = kg_ref[...]
    for hd in range(N_KV_HEADS):
        o1, o2 = norm_rope(yt[hd * HEAD_DIM:(hd + 1) * HEAD_DIM], kg)
        k_ref[hd] = jnp.concatenate([o1, o2], axis=0).T.astype(BF16)
        v_t = yt[KV_DIM + hd * HEAD_DIM:KV_DIM + (hd + 1) * HEAD_DIM]
        vt_ref[hd] = v_t.astype(BF16)


def _qkv(x2d, seq, tm, mod, mod_tiles_per_row, norm_g, wqkv_t, q_g, k_g, cos_t, sin_t):
    t = x2d.shape[0]
    tpb = seq // tm
    const = lambda i: (0, 0)
    return pl.pallas_call(
        _qkv_kernel,
        grid=(t // tm,),
        in_specs=[
            pl.BlockSpec((tm, D_MODEL), lambda i: (i, 0)),
            pl.BlockSpec((1, D_MODEL), const),
            _mod_spec(0, mod_tiles_per_row),
            _mod_spec(1, mod_tiles_per_row),
            _resident((QKV_DIM, D_MODEL), const),
            pl.BlockSpec((HEAD_DIM, 1), const),
            pl.BlockSpec((HEAD_DIM, 1), const),
            pl.BlockSpec((HEAD_DIM // 2, tm), lambda i: (0, i % tpb)),
            pl.BlockSpec((HEAD_DIM // 2, tm), lambda i: (0, i % tpb)),
        ],
        out_specs=[
            pl.BlockSpec((Q_DIM, tm), lambda i: (0, i)),
            pl.BlockSpec((None, N_KV_HEADS, tm, HEAD_DIM), lambda i: (i // tpb, 0, i % tpb, 0)),
            pl.BlockSpec((None, N_KV_HEADS, HEAD_DIM, tm), lambda i: (i // tpb, 0, 0, i % tpb)),
        ],
        out_shape=[
            jax.ShapeDtypeStruct((Q_DIM, t), BF16),
            jax.ShapeDtypeStruct((t // seq, N_KV_HEADS, seq, HEAD_DIM), BF16),
            jax.ShapeDtypeStruct((t // seq, N_KV_HEADS, HEAD_DIM, seq), BF16),
        ],
        compiler_params=_params(1),
        name="qkv_proj",
    )(x2d, norm_g, mod, mod, wqkv_t, q_g, k_g, cos_t, sin_t)


MAX_TILES_PER_HEAD = 16
L_MIN = 2.0 ** -60


def _flash_kernel(*refs, tiles, n_cast):
    n_src = len(tiles)
    n_in = 1 + 2 * n_src + n_cast
    q_ref = refs[0]
    k_refs = refs[1:1 + 2 * n_src:2]
    vt_refs = refs[2:2 + 2 * n_src:2]
    o_ref = refs[n_in]
    shift_scr, acc_scr, l_scr, kmax_scr = refs[n_in + 1 + n_cast:]

    for w_ref, w_bf16_ref in zip(refs[n_in - n_cast:n_in], refs[n_in + 1:n_in + 1 + n_cast]):
        w_bf16_ref[...] = w_ref[...].astype(BF16)

    @pl.when(pl.program_id(2) == 0)
    def _():
        kmax2 = jnp.zeros((1, 1), F32)
        for k_ref, (tk, nk) in zip(k_refs, tiles):
            def body(t, n2, k_ref=k_ref, tk=tk):
                kk = k_ref[pl.ds(pl.multiple_of(t * tk, tk), tk), :].astype(F32)
                return jnp.maximum(n2, jnp.sum(kk * kk, axis=1, keepdims=True))

            n2 = lax.fori_loop(0, nk, body, jnp.zeros((tk, 1), F32))
            kmax2 = jnp.maximum(kmax2, jnp.max(n2, axis=0, keepdims=True))
        kmax_scr[...] = jnp.broadcast_to(jnp.sqrt(kmax2), kmax_scr.shape)

    def operands(src, g, t):
        tk = tiles[src][0]
        ks = t * tk if isinstance(t, int) else pl.multiple_of(t * tk, tk)
        qs = g * HEAD_DIM if isinstance(g, int) else pl.multiple_of(g * HEAD_DIM, HEAD_DIM)
        q_t = q_ref[pl.ds(qs, HEAD_DIM), :]
        return k_refs[src][pl.ds(ks, tk), :], q_t, vt_refs[src][:, pl.ds(ks, tk)]

    def finalize():
        l_min = None
        for g in range(GROUP):
            l = l_scr[g]
            o = acc_scr[g] * (1.0 / l)
            o_ref[:, g * HEAD_DIM:(g + 1) * HEAD_DIM] = o.astype(BF16).T
            l_min = l if l_min is None else jnp.minimum(l_min, l)
        return jnp.min(l_min)

    for g in range(GROUP):
        qf = q_ref[g * HEAD_DIM:(g + 1) * HEAD_DIM, :].astype(F32)
        shift_scr[g] = jnp.sqrt(jnp.sum(qf * qf, axis=0, keepdims=True)) * kmax_scr[...]

    def bounded_step(src, g, t):
        k_tile, q_t, vt_tile = operands(src, g, t)
        s = jnp.dot(k_tile, q_t, preferred_element_type=F32)
        p = jnp.exp2(s - shift_scr[g])
        l_tile = jnp.sum(p, axis=0, keepdims=True)
        pv = jnp.dot(vt_tile, p.astype(BF16), preferred_element_type=F32)
        if (src, t) == (0, 0):
            acc_scr[g], l_scr[g] = pv, l_tile
        else:
            acc_scr[g] += pv
            l_scr[g] += l_tile

    def per_head(step_fn):
        def body(g, carry):
            for src, (_, nk) in enumerate(tiles):
                for t in range(nk):
                    step_fn(src, g, t)
            return carry

        lax.fori_loop(0, GROUP, body, 0)

    per_head(bounded_step)
    l_min = finalize()

    @pl.when(jnp.logical_not(l_min >= L_MIN))
    def _():
        shift_scr[...] = jnp.full(shift_scr.shape, NEG_BIG, F32)
        acc_scr[...] = jnp.zeros(acc_scr.shape, F32)
        l_scr[...] = jnp.zeros(l_scr.shape, F32)

        def online_step(src, g, t):
            k_tile, q_t, vt_tile = operands(src, g, t)
            s = jnp.dot(k_tile, q_t, preferred_element_type=F32)
            m_old = shift_scr[g]
            m_new = jnp.maximum(m_old, jnp.max(s, axis=0, keepdims=True))
            shift_scr[g] = m_new
            alpha = jnp.exp2(m_old - m_new)
            p = jnp.exp2(s - m_new)
            l_scr[g] = alpha * l_scr[g] + jnp.sum(p, axis=0, keepdims=True)
            pv = jnp.dot(vt_tile, p.astype(BF16), preferred_element_type=F32)
            acc_scr[g] = alpha * acc_scr[g] + pv

        for src, (_, nk) in enumerate(tiles):
            def body(u, carry, src=src, nk=nk):
                g = u // nk
                online_step(src, g, u - g * nk)
                return carry

            lax.fori_loop(0, GROUP * nk, body, 0)
        finalize()


def _flash(q_t, sources, seq_q, tq, tk, cast=()):
    b = sources[0][0].shape[0]
    nq = seq_q // tq
    gw = GROUP * HEAD_DIM
    n_grid = b * N_KV_HEADS * nq
    grid_pos = lambda bi, h, i: ((bi * N_KV_HEADS + h) * nq + i, 0)
    cast_specs = []
    for w in cast:
        rows, rem = divmod(w.shape[0], n_grid)
        assert rem == 0 and rows % BF16_SUBLANE_TILE == 0, w.shape
        cast_specs.append(pl.BlockSpec((rows, w.shape[1]), grid_pos))
    kv_args, kv_specs, tiles = [], [], []
    for k, vt in sources:
        lk = k.shape[2]
        rows = min(tk, lk)
        tiles.append((rows, lk // rows))
        kv_args += [k, vt]
        assert sum(nk for _, nk in tiles) <= MAX_TILES_PER_HEAD, tiles
        kv_specs += [pl.BlockSpec((None, None, lk, HEAD_DIM), lambda bi, h, i: (bi, h, 0, 0)),
                     pl.BlockSpec((None, None, HEAD_DIM, lk), lambda bi, h, i: (bi, h, 0, 0))]
    out = pl.pallas_call(
        functools.partial(_flash_kernel, tiles=tuple(tiles), n_cast=len(cast)),
        grid=(b, N_KV_HEADS, nq),
        in_specs=([pl.BlockSpec((gw, tq), lambda bi, h, i: (h, bi * nq + i))] + kv_specs
                  + cast_specs),
        out_specs=[pl.BlockSpec((tq, gw), lambda bi, h, i: (bi * nq + i, h))] + cast_specs,
        out_shape=([jax.ShapeDtypeStruct((q_t.shape[1], q_t.shape[0]), BF16)]
                   + [jax.ShapeDtypeStruct(w.shape, BF16) for w in cast]),
        scratch_shapes=[pltpu.VMEM((GROUP, 1, tq), F32), pltpu.VMEM((GROUP, HEAD_DIM, tq), F32),
                        pltpu.VMEM((GROUP, 1, tq), F32), pltpu.VMEM((1, tq), F32)],
        compiler_params=pltpu.CompilerParams(
            dimension_semantics=("parallel", "parallel", "arbitrary"),
            vmem_limit_bytes=VMEM_LIMIT_BYTES),
        name="flash_attn",
    )(q_t, *kv_args, *cast)
    return out[0], out[1:]


MLP_SUB_ROWS = 512


def _mlp_kernel(*refs, with_proj, with_final_norm, ff_chunk):
    refs = list(refs)
    x_ref = refs.pop(0)
    if with_proj:
        o_ref, wo_ref, gate_mix_ref = refs.pop(0), refs.pop(0), refs.pop(0)
    g_ref, shift_ref, scale_ref, gate_ref, w1_ref, w2_ref = refs[:6]
    refs = refs[6:]
    if with_final_norm:
        fg_ref = refs.pop(0)
    (out_ref,) = refs

    sub = min(MLP_SUB_ROWS, x_ref.shape[0])
    for r0 in range(0, x_ref.shape[0], sub):
        rows = slice(r0, r0 + sub)
        x = x_ref[rows, :]
        if with_proj:
            x = x + gate_mix_ref[...] * jnp.dot(o_ref[rows, :], wo_ref[...],
                                                preferred_element_type=F32)
        h = _rms_mod(x, g_ref[...], shift_ref[...], scale_ref[...]).astype(BF16)
        y = jnp.zeros_like(x)
        for c in range(D_FF // ff_chunk):
            sl = slice(c * ff_chunk, (c + 1) * ff_chunk)
            a = jnp.maximum(jnp.dot(h, w1_ref[:, sl], preferred_element_type=F32), 0.0)
            y = y + jnp.dot((a * a).astype(BF16), w2_ref[sl, :], preferred_element_type=F32)
        x = x + gate_ref[...] * y
        if with_final_norm:
            x = x * lax.rsqrt(jnp.mean(x * x, axis=-1, keepdims=True) + EPS) * fg_ref[...]
        out_ref[rows, :] = x


def _mlp(x2d, tm, mod, mod_tiles_per_row, norm_g, w1, w2, layer, *, attn_o=None, wo=None,
         final_g=None):
    t = x2d.shape[0]
    const = lambda i: (0, 0)
    this_layer = lambda i: (layer, 0, 0)
    tile = pl.BlockSpec((tm, D_MODEL), lambda i: (i, 0))
    args, specs = [x2d], [tile]
    if attn_o is not None:
        args += [attn_o, wo, mod]
        specs += [pl.BlockSpec((tm, Q_DIM), lambda i: (i, 0)), _resident((Q_DIM, D_MODEL), const),
                  _mod_spec(2, mod_tiles_per_row)]
    args += [norm_g, mod, mod, mod, w1, w2]
    specs += [pl.BlockSpec((1, D_MODEL), const), _mod_spec(3, mod_tiles_per_row),
              _mod_spec(4, mod_tiles_per_row), _mod_spec(5, mod_tiles_per_row),
              _resident((None, D_MODEL, D_FF), this_layer),
              _resident((None, D_FF, D_MODEL), this_layer)]
    if final_g is not None:
        args.append(final_g)
        specs.append(pl.BlockSpec((1, D_MODEL), const))
    return pl.pallas_call(
        functools.partial(_mlp_kernel, with_proj=attn_o is not None,
                          with_final_norm=final_g is not None, ff_chunk=1024),
        grid=(t // tm,),
        in_specs=specs,
        out_specs=tile,
        out_shape=jax.ShapeDtypeStruct(x2d.shape, F32),
        compiler_params=_params(1),
        name="mlp",
    )(*args)


def _gelu(z):
    return 0.5 * z * (1.0 + lax.erf(z * float(np.sqrt(0.5))))


SGU_COL_BLOCK = 2 * SGU_GROUP_DIM


def _sgu_kernel(x_ref, g_ref, shift_ref, scale_ref, gate_ref, win_ref, bin_ref, vg_ref, ws_ref,
                bs_ref, wout_ref, out_ref, v_scr, t_scr):
    tm = x_ref.shape[0]
    x = x_ref[...]
    h = _rms_mod(x, g_ref[...], shift_ref[...], scale_ref[...]).astype(BF16)
    zv = _gelu(jnp.dot(h, win_ref[:, SGU_DIM:], preferred_element_type=F32) + bin_ref[:, SGU_DIM:])
    r = lax.rsqrt(jnp.mean(zv * zv, axis=-1, keepdims=True) + EPS)
    v_scr[...] = (zv * r * vg_ref[...]).astype(BF16)
    for cb in range(SGU_DIM // SGU_COL_BLOCK):
        cols = slice(cb * SGU_COL_BLOCK, (cb + 1) * SGU_COL_BLOCK)
        u = _gelu(jnp.dot(h, win_ref[:, cols], preferred_element_type=F32) + bin_ref[:, cols])
        for gi in range(SGU_COL_BLOCK // SGU_GROUP_DIM):
            grp = cb * (SGU_COL_BLOCK // SGU_GROUP_DIM) + gi
            w_s = ws_ref[grp]
            b_s = bs_ref[grp]
            for c in range(tm // CHUNK):
                rows = slice(c * CHUNK, (c + 1) * CHUNK)
                gcols = slice(grp * SGU_GROUP_DIM, (grp + 1) * SGU_GROUP_DIM)
                sv = jnp.dot(w_s, v_scr[rows, gcols], preferred_element_type=F32) + b_s
                lcols = slice(gi * SGU_GROUP_DIM, (gi + 1) * SGU_GROUP_DIM)
                t_scr[rows, gcols] = (u[rows, lcols] * sv).astype(BF16)
    y = jnp.dot(t_scr[...], wout_ref[...], preferred_element_type=F32)
    out_ref[...] = x + gate_ref[...] * y


def _sgu(x2d, tm, mod, mod_tiles_per_row, norm_g, w_in, b_in, v_g, w_s, b_s, w_out):
    t = x2d.shape[0]
    const = lambda i: (0, 0)
    const3 = lambda i: (0, 0, 0)
    tile = pl.BlockSpec((tm, D_MODEL), lambda i: (i, 0))
    return pl.pallas_call(
        _sgu_kernel,
        grid=(t // tm,),
        in_specs=[
            tile,
            pl.BlockSpec((1, D_MODEL), const),
            _mod_spec(0, mod_tiles_per_row),
            _mod_spec(1, mod_tiles_per_row),
            _mod_spec(2, mod_tiles_per_row),
            _resident((D_MODEL, 2 * SGU_DIM), const),
            pl.BlockSpec((1, 2 * SGU_DIM), const),
            pl.BlockSpec((1, SGU_DIM), const),
            pl.BlockSpec((SGU_GROUPS, CHUNK, CHUNK), const3),
            pl.BlockSpec((SGU_GROUPS, CHUNK, 1), const3),
            _resident((SGU_DIM, D_MODEL), const),
        ],
        out_specs=tile,
        out_shape=jax.ShapeDtypeStruct(x2d.shape, F32),
        scratch_shapes=[pltpu.VMEM((tm, SGU_DIM), BF16), pltpu.VMEM((tm, SGU_DIM), BF16)],
        compiler_params=_params(1),
        name="sgu",
    )(x2d, norm_g, mod, mod, mod, w_in, b_in, v_g, w_s, b_s, w_out)


def _pairs_apart(w, n_heads):
    split = w.reshape(n_heads, HEAD_DIM // 2, 2, *w.shape[1:])
    return jnp.swapaxes(split, 1, 2).reshape(w.shape)


def kernel(x, c, ctx, c_ctx, ada_w, ada_b, mix_norm_g, mlp_norm_g, mlp_w1, mlp_w2, attn_wqkv,
           attn_q_g, attn_k_g, attn_wo, sgu_w_in, sgu_b_in, sgu_v_g, sgu_w_s, sgu_b_s, sgu_w_out,
           final_g):
    b, n, d = x.shape
    lc = ctx.shape[1]
    tm_x, tm_mlp, tm_sgu, tq, tk = QKV_TILE, MLP_TILE, SGU_TILE, FLASH_Q_TILE, FLASH_K_TILE
    tm_c = lc

    cond = jnp.concatenate([c, c_ctx[None, :], jnp.zeros((MOD_ROWS - b - 1, d), F32)], axis=0)
    mods = _adaln(cond, ada_w, ada_b)
    cos_t, sin_t = _rope_tables(n)
    cos_c = jnp.ones((HEAD_DIM // 2, lc), F32)
    sin_c = jnp.zeros((HEAD_DIM // 2, lc), F32)

    x2 = x.reshape(b * n, d)
    c2 = ctx.reshape(b * lc, d)
    row = lambda v: v.reshape(1, -1)
    stacked = dict(w1=mlp_w1, w2=mlp_w2, wo=attn_wo, w_in=sgu_w_in, w_out=sgu_w_out)
    bf16 = None
    for i in range(DEPTH):
        last = i == DEPTH - 1
        use_attn = (i % N_MIXERS) == 0
        j = i // N_MIXERS
        mod_x = mods[i, :b].reshape(b, 1, N_MOD * d)
        mod_c = mods[i, b:b + 1].reshape(1, 1, N_MOD * d)
        x_tpr, c_tpr = n // tm_x, (b * lc) // tm_c
        mix_g = row(mix_norm_g[i])
        mlp_g = row(mlp_norm_g[i])
        fg = row(final_g) if last else None
        if use_attn:
            w_t = attn_wqkv[j].T
            wqkv_t = jnp.concatenate(
                [_pairs_apart(w_t[:Q_DIM + KV_DIM], N_HEADS + N_KV_HEADS), w_t[Q_DIM + KV_DIM:]],
                axis=0).astype(BF16)
            q_g = _pairs_apart(attn_q_g[j], 1)[:, None]
            k_g = _pairs_apart(attn_k_g[j], 1)[:, None]
            q_t, k, vt = _qkv(x2, n, tm_x, mod_x, x_tpr, mix_g, wqkv_t, q_g, k_g, cos_t, sin_t)
            qc_t, kc, vtc = _qkv(c2, lc, tm_c, mod_c, c_tpr, mix_g, wqkv_t, q_g, k_g,
                                 cos_c, sin_c)
            to_cast = () if bf16 else tuple(w.reshape(-1, w.shape[-1]) for w in stacked.values())
            o, casted = _flash(q_t, [(k, vt), (kc, vtc)], n, tq, tk, cast=to_cast)
            if not bf16:
                bf16 = {name: wb.reshape(w.shape)
                        for (name, w), wb in zip(stacked.items(), casted)}
            w1, w2, wo = bf16["w1"], bf16["w2"], bf16["wo"][j]
            x2 = _mlp(x2, tm_mlp, mod_x, n // tm_mlp, mlp_g, w1, w2, i, attn_o=o, wo=wo,
                      final_g=fg)
            if not last:
                oc, _ = _flash(qc_t, [(kc, vtc)], lc, lc, tk)
                c2 = _mlp(c2, tm_c, mod_c, c_tpr, mlp_g, w1, w2, i, attn_o=oc, wo=wo)
        else:
            w1, w2 = bf16["w1"], bf16["w2"]
            sgu_args = (bf16["w_in"][j], row(sgu_b_in[j]), row(sgu_v_g[j]),
                        sgu_w_s[j].astype(BF16), sgu_b_s[j][:, :, None], bf16["w_out"][j])
            x2 = _sgu(x2, tm_sgu, mod_x, n // tm_sgu, mix_g, *sgu_args)
            x2 = _mlp(x2, tm_mlp, mod_x, n // tm_mlp, mlp_g, w1, w2, i, final_g=fg)
            if not last:
                c2 = _sgu(c2, tm_c, mod_c, c_tpr, mix_g, *sgu_args)
                c2 = _mlp(c2, tm_c, mod_c, c_tpr, mlp_g, w1, w2, i)
    return x2.reshape(b, n, d)
```

```python
import functools

import jax
import jax.numpy as jnp
import numpy as np
from jax import lax
from jax.experimental import pallas as pl
from jax.experimental.pallas import tpu as pltpu

D_MODEL = 1024
DEPTH = 2
GRID_W = 64
N_MIXERS = 2
N_HEADS = 8
N_KV_HEADS = 2
HEAD_DIM = 128
GROUP = N_HEADS // N_KV_HEADS
Q_DIM = N_HEADS * HEAD_DIM
KV_DIM = N_KV_HEADS * HEAD_DIM
QKV_DIM = Q_DIM + 2 * KV_DIM
ROPE_AXIS_DIM = HEAD_DIM // 2
ROPE_THETA = 10000.0
CHUNK = 128
SGU_DIM = 3 * D_MODEL
SGU_GROUPS = 8
SGU_GROUP_DIM = SGU_DIM // SGU_GROUPS
D_FF = 4 * D_MODEL
N_MOD = 6
EPS = 1e-6

F32 = jnp.float32
BF16 = jnp.bfloat16

VMEM_LIMIT_BYTES = 56 * 1024 * 1024
MOD_ROWS = 8

QKV_TILE = 512
MLP_TILE = 1024
SGU_TILE = 4 * CHUNK
FLASH_Q_TILE = 1024
FLASH_K_TILE = 4096

SOFTMAX_Q_SCALE = float(HEAD_DIM ** -0.5 * np.log2(np.e))
NEG_BIG = -1e30
BF16_SUBLANE_TILE = 16


def _params(n_axes):
    return pltpu.CompilerParams(dimension_semantics=("parallel",) * n_axes,
                                vmem_limit_bytes=VMEM_LIMIT_BYTES)


def _resident(shape, index_map):
    return pl.BlockSpec(shape, index_map, pipeline_mode=pl.Buffered(1))


def _rms_mod(x, g, shift, scale):
    gain = g * (1.0 + scale)
    return x * lax.rsqrt(jnp.mean(x * x, axis=-1, keepdims=True) + EPS) * gain + shift


def _mod_spec(k, tiles_per_row):
    return pl.BlockSpec((None, 1, D_MODEL), lambda i: (i // tiles_per_row, 0, k))


def _adaln_kernel(c_ref, w_ref, b_ref, o_ref):
    c = c_ref[...]
    s = c * jax.nn.sigmoid(c)
    o_ref[...] = jnp.dot(s.astype(BF16), w_ref[...].astype(BF16),
                         preferred_element_type=F32) + b_ref[...]


def _adaln(cond, ada_w, ada_b):
    depth = ada_w.shape[0]
    return pl.pallas_call(
        _adaln_kernel,
        grid=(depth, N_MOD),
        in_specs=[
            pl.BlockSpec((MOD_ROWS, D_MODEL), lambda i, j: (0, 0)),
            pl.BlockSpec((None, D_MODEL, D_MODEL), lambda i, j: (i, 0, j)),
            pl.BlockSpec((None, 1, D_MODEL), lambda i, j: (i, 0, j)),
        ],
        out_specs=pl.BlockSpec((None, MOD_ROWS, D_MODEL), lambda i, j: (i, 0, j)),
        out_shape=jax.ShapeDtypeStruct((depth, MOD_ROWS, N_MOD * D_MODEL), F32),
        compiler_params=_params(2),
        name="adaln",
    )(cond, ada_w, ada_b.reshape(depth, 1, N_MOD * D_MODEL))


def _cos_sin_kernel(ang_ref, cos_ref, sin_ref):
    a = ang_ref[...]
    cos_ref[...] = jnp.cos(a)
    sin_ref[...] = jnp.sin(a)


def _rope_tables(n):
    rows_count = n // GRID_W
    freqs = 1.0 / (ROPE_THETA ** (jnp.arange(0, ROPE_AXIS_DIM, 2, dtype=F32) / ROPE_AXIS_DIM))
    pos = jnp.arange(rows_count + GRID_W, dtype=jnp.int32)
    pos = jnp.where(pos < rows_count, pos, pos - rows_count).astype(F32)
    ang = pos[:, None] * freqs[None, :]
    cos_s, sin_s = pl.pallas_call(
        _cos_sin_kernel,
        out_shape=(jax.ShapeDtypeStruct(ang.shape, F32),) * 2,
        name="rope_cos_sin",
    )(ang)

    def expand(t):
        by_row = jnp.repeat(t[:rows_count], GRID_W, axis=0)
        by_col = jnp.tile(t[rows_count:], (rows_count, 1))
        return jnp.concatenate([by_row, by_col], axis=-1)

    return expand(cos_s).T, expand(sin_s).T


Q_HEADS_PER_DOT = 4


def _qkv_kernel(x_ref, g_ref, shift_ref, scale_ref, wt_ref, qg_ref, kg_ref, cos_ref, sin_ref,
                qt_ref, k_ref, vt_ref):
    h =_rms_mod(x_ref[...], g_ref[...], shift_ref[...], scale_ref[...]).astype(BF16)
    cos = cos_ref[...]
    sin = sin_ref[...]
    half = HEAD_DIM // 2
    nt_dims = (((1,), (1,)), ((), ()))

    def norm_rope(t, g):
        t = t * lax.rsqrt(jnp.mean(t * t, axis=0, keepdims=True) + EPS) * g
        x1, x2 = t[:half], t[half:]
        return x1 * cos - x2 * sin, x1 * sin + x2 * cos

    qg = qg_ref[...]
    for blk in range(N_HEADS // Q_HEADS_PER_DOT):
        rows = slice(blk * Q_HEADS_PER_DOT * HEAD_DIM, (blk + 1) * Q_HEADS_PER_DOT * HEAD_DIM)
        yt = lax.dot_general(wt_ref[rows, :], h, nt_dims, preferred_element_type=F32)
        for j in range(Q_HEADS_PER_DOT):
            r0 = (blk * Q_HEADS_PER_DOT + j) * HEAD_DIM
            o1, o2 = norm_rope(yt[j * HEAD_DIM:(j + 1) * HEAD_DIM], qg)
            qt_ref[r0:r0 + half, :] = (o1 * SOFTMAX_Q_SCALE).astype(BF16)
            qt_ref[r0 + half:r0 + HEAD_DIM, :] = (o2 * SOFTMAX_Q_SCALE).astype(BF16)
    yt = lax.dot_general(wt_ref[Q_DIM:, :], h, nt_dims, preferred_element_type=F32)
    kg = kg_ref[...]
    for hd in range(N_KV_HEADS):
        o1, o2 = norm_rope(yt[hd * HEAD_DIM:(hd + 1) * HEAD_DIM], kg)
        k_ref[hd] = jnp.concatenate([o1, o2], axis=0).T.astype(BF16)
        v_t = yt[KV_DIM + hd * HEAD_DIM:KV_DIM + (hd + 1) * HEAD_DIM]
        vt_ref[hd] = v_t.astype(BF16)


def _qkv(x2d, seq, tm, mod, mod_tiles_per_row, norm_g, wqkv_t, q_g, k_g, cos_t, sin_t):
    t = x2d.shape[0]
    tpb = seq // tm
    const = lambda i: (0, 0)
    return pl.pallas_call(
        _qkv_kernel,
        grid=(t // tm,),
        in_specs=[
            pl.BlockSpec((tm, D_MODEL), lambda i: (i, 0)),
            pl.BlockSpec((1, D_MODEL), const),
            _mod_spec(0, mod_tiles_per_row),
            _mod_spec(1, mod_tiles_per_row),
            _resident((QKV_DIM, D_MODEL), const),
            pl.BlockSpec((HEAD_DIM, 1), const),
            pl.BlockSpec((HEAD_DIM, 1), const),
            pl.BlockSpec((HEAD_DIM // 2, tm), lambda i: (0, i % tpb)),
            pl.BlockSpec((HEAD_DIM // 2, tm), lambda i: (0, i % tpb)),
        ],
        out_specs=[
            pl.BlockSpec((Q_DIM, tm), lambda i: (0, i)),
            pl.BlockSpec((None, N_KV_HEADS, tm, HEAD_DIM), lambda i: (i // tpb, 0, i % tpb, 0)),
            pl.BlockSpec((None, N_KV_HEADS, HEAD_DIM, tm), lambda i: (i // tpb, 0, 0, i % tpb)),
        ],
        out_shape=[
            jax.ShapeDtypeStruct((Q_DIM, t), BF16),
            jax.ShapeDtypeStruct((t // seq, N_KV_HEADS, seq, HEAD_DIM), BF16),
            jax.ShapeDtypeStruct((t // seq, N_KV_HEADS, HEAD_DIM, seq), BF16),
        ],
        compiler_params=_params(1),
        name="qkv_proj",
    )(x2d, norm_g, mod, mod, wqkv_t, q_g, k_g, cos_t, sin_t)


MAX_TILES_PER_HEAD = 16
L_MIN = 2.0 ** -60


def _flash_kernel(*refs, tiles, n_cast):
    n_src = len(tiles)
    n_in = 1 + 2 * n_src + n_cast
    q_ref = refs[0]
    k_refs = refs[1:1 + 2 * n_src:2]
    vt_refs = refs[2:2 + 2 * n_src:2]
    o_ref = refs[n_in]
    shift_scr, acc_scr, l_scr, kmax_scr = refs[n_in + 1 + n_cast:]

    for w_ref, w_bf16_ref in zip(refs[n_in - n_cast:n_in], refs[n_in + 1:n_in + 1 + n_cast]):
        w_bf16_ref[...] = w_ref[...].astype(BF16)

    @pl.when(pl.program_id(2) == 0)
    def _():
        kmax2 = jnp.zeros((1, 1), F32)
        for k_ref, (tk, nk) in zip(k_refs, tiles):
            def body(t, n2, k_ref=k_ref, tk=tk):
                kk = k_ref[pl.ds(pl.multiple_of(t * tk, tk), tk), :].astype(F32)
                return jnp.maximum(n2, jnp.sum(kk * kk, axis=1, keepdims=True))

            n2 = lax.fori_loop(0, nk, body, jnp.zeros((tk, 1), F32))
            kmax2 = jnp.maximum(kmax2, jnp.max(n2, axis=0, keepdims=True))
        kmax_scr[...] = jnp.broadcast_to(jnp.sqrt(kmax2), kmax_scr.shape)

    def operands(src, g, t):
        tk = tiles[src][0]
        ks = t * tk if isinstance(t, int) else pl.multiple_of(t * tk, tk)
        qs = g * HEAD_DIM if isinstance(g, int) else pl.multiple_of(g * HEAD_DIM, HEAD_DIM)
        q_t = q_ref[pl.ds(qs, HEAD_DIM), :]
        return k_refs[src][pl.ds(ks, tk), :], q_t, vt_refs[src][:, pl.ds(ks, tk)]

    def finalize():
        l_min = None
        for g in range(GROUP):
            l = l_scr[g]
            o = acc_scr[g] * (1.0 / l)
            o_ref[:, g * HEAD_DIM:(g + 1) * HEAD_DIM] = o.astype(BF16).T
            l_min = l if l_min is None else jnp.minimum(l_min, l)
        return jnp.min(l_min)

    for g in range(GROUP):
        qf = q_ref[g * HEAD_DIM:(g + 1) * HEAD_DIM, :].astype(F32)
        shift_scr[g] = jnp.sqrt(jnp.sum(qf * qf, axis=0, keepdims=True)) * kmax_scr[...]

    def bounded_step(src, g, t):
        k_tile, q_t, vt_tile = operands(src, g, t)
        s = jnp.dot(k_tile, q_t, preferred_element_type=F32)
        p = jnp.exp2(s - shift_scr[g])
        l_tile = jnp.sum(p, axis=0, keepdims=True)
        pv = jnp.dot(vt_tile, p.astype(BF16), preferred_element_type=F32)
        if (src, t) == (0, 0):
            acc_scr[g], l_scr[g] = pv, l_tile
        else:
            acc_scr[g] += pv
            l_scr[g] += l_tile

    def per_head(step_fn):
        def body(g, carry):
            for src, (_, nk) in enumerate(tiles):
                for t in range(nk):
                    step_fn(src, g, t)
            return carry

        lax.fori_loop(0, GROUP, body, 0)

    per_head(bounded_step)
    l_min = finalize()

    @pl.when(jnp.logical_not(l_min >= L_MIN))
    def _():
        shift_scr[...] = jnp.full(shift_scr.shape, NEG_BIG, F32)
        acc_scr[...] = jnp.zeros(acc_scr.shape, F32)
        l_scr[...] = jnp.zeros(l_scr.shape, F32)

        def online_step(src, g, t):
            k_tile, q_t, vt_tile = operands(src, g, t)
            s = jnp.dot(k_tile, q_t, preferred_element_type=F32)
            m_old = shift_scr[g]
            m_new = jnp.maximum(m_old, jnp.max(s, axis=0, keepdims=True))
            shift_scr[g] = m_new
            alpha = jnp.exp2(m_old - m_new)
            p = jnp.exp2(s - m_new)
            l_scr[g] = alpha * l_scr[g] + jnp.sum(p, axis=0, keepdims=True)
            pv = jnp.dot(vt_tile, p.astype(BF16), preferred_element_type=F32)
            acc_scr[g] = alpha * acc_scr[g] + pv

        for src, (_, nk) in enumerate(tiles):
            def body(u, carry, src=src, nk=nk):
                g = u // nk
                online_step(src, g, u - g * nk)
                return carry

            lax.fori_loop(0, GROUP * nk, body, 0)
        finalize()


def _flash(q_t, sources, seq_q, tq, tk, cast=()):
    b = sources[0][0].shape[0]
    nq = seq_q // tq
    gw = GROUP * HEAD_DIM
    n_grid = b * N_KV_HEADS * nq
    grid_pos = lambda bi, h, i: ((bi * N_KV_HEADS + h) * nq + i, 0)
    cast_specs = []
    for w in cast:
        rows, rem = divmod(w.shape[0], n_grid)
        assert rem == 0 and rows % BF16_SUBLANE_TILE == 0, w.shape
        cast_specs.append(pl.BlockSpec((rows, w.shape[1]), grid_pos))
    kv_args, kv_specs, tiles = [], [], []
    for k, vt in sources:
        lk = k.shape[2]
        rows = min(tk, lk)
        tiles.append((rows, lk // rows))
        kv_args += [k, vt]
        assert sum(nk for _, nk in tiles) <= MAX_TILES_PER_HEAD, tiles
        kv_specs += [pl.BlockSpec((None, None, lk, HEAD_DIM), lambda bi, h, i: (bi, h, 0, 0)),
                     pl.BlockSpec((None, None, HEAD_DIM, lk), lambda bi, h, i: (bi, h, 0, 0))]
    out = pl.pallas_call(
        functools.partial(_flash_kernel, tiles=tuple(tiles), n_cast=len(cast)),
        grid=(b, N_KV_HEADS, nq),
        in_specs=([pl.BlockSpec((gw, tq), lambda bi, h, i: (h, bi * nq + i))] + kv_specs
                  + cast_specs),
        out_specs=[pl.BlockSpec((tq, gw), lambda bi, h, i: (bi * nq + i, h))] + cast_specs,
        out_shape=([jax.ShapeDtypeStruct((q_t.shape[1], q_t.shape[0]), BF16)]
                   + [jax.ShapeDtypeStruct(w.shape, BF16) for w in cast]),
        scratch_shapes=[pltpu.VMEM((GROUP, 1, tq), F32), pltpu.VMEM((GROUP, HEAD_DIM, tq), F32),
                        pltpu.VMEM((GROUP, 1, tq), F32), pltpu.VMEM((1, tq), F32)],
        compiler_params=pltpu.CompilerParams(
            dimension_semantics=("parallel", "parallel", "arbitrary"),
            vmem_limit_bytes=VMEM_LIMIT_BYTES),
        name="flash_attn",
    )(q_t, *kv_args, *cast)
    return out[0], out[1:]


MLP_SUB_ROWS = 512


def _mlp_kernel(*refs, with_proj, with_final_norm, ff_chunk):
    refs = list(refs)
    x_ref = refs.pop(0)
    if with_proj:
        o_ref, wo_ref, gate_mix_ref = refs.pop(0), refs.pop(0), refs.pop(0)
    g_ref, shift_ref, scale_ref, gate_ref, w1_ref, w2_ref = refs[:6]
    refs = refs[6:]
    if with_final_norm:
        fg_ref = refs.pop(0)
    (out_ref,) = refs

    sub = min(MLP_SUB_ROWS, x_ref.shape[0])
    for r0 in range(0, x_ref.shape[0], sub):
        rows = slice(r0, r0 + sub)
        x = x_ref[rows, :]
        if with_proj:
            x = x + gate_mix_ref[...] * jnp.dot(o_ref[rows, :], wo_ref[...],
                                                preferred_element_type=F32)
        h = _rms_mod(x, g_ref[...], shift_ref[...], scale_ref[...]).astype(BF16)
        y = jnp.zeros_like(x)
        for c in range(D_FF // ff_chunk):
            sl = slice(c * ff_chunk, (c + 1) * ff_chunk)
            a = jnp.maximum(jnp.dot(h, w1_ref[:, sl], preferred_element_type=F32), 0.0)
            y = y + jnp.dot((a * a).astype(BF16), w2_ref[sl, :], preferred_element_type=F32)
        x = x + gate_ref[...] * y
        if with_final_norm:
            x = x * lax.rsqrt(jnp.mean(x * x, axis=-1, keepdims=True) + EPS) * fg_ref[...]
        out_ref[rows, :] = x


def _mlp(x2d, tm, mod, mod_tiles_per_row, norm_g, w1, w2, layer, *, attn_o=None, wo=None,
         final_g=None):
    t = x2d.shape[0]
    const = lambda i: (0, 0)
    this_layer = lambda i: (layer, 0, 0)
    tile = pl.BlockSpec((tm, D_MODEL), lambda i: (i, 0))
    args, specs = [x2d], [tile]
    if attn_o is not None:
        args += [attn_o, wo, mod]
        specs += [pl.BlockSpec((tm, Q_DIM), lambda i: (i, 0)), _resident((Q_DIM, D_MODEL), const),
                  _mod_spec(2, mod_tiles_per_row)]
    args += [norm_g, mod, mod, mod, w1, w2]
    specs += [pl.BlockSpec((1, D_MODEL), const), _mod_spec(3, mod_tiles_per_row),
              _mod_spec(4, mod_tiles_per_row), _mod_spec(5, mod_tiles_per_row),
              _resident((None, D_MODEL, D_FF), this_layer),
              _resident((None, D_FF, D_MODEL), this_layer)]
    if final_g is not None:
        args.append(final_g)
        specs.append(pl.BlockSpec((1, D_MODEL), const))
    return pl.pallas_call(
        functools.partial(_mlp_kernel, with_proj=attn_o is not None,
                          with_final_norm=final_g is not None, ff_chunk=1024),
        grid=(t // tm,),
        in_specs=specs,
        out_specs=tile,
        out_shape=jax.ShapeDtypeStruct(x2d.shape, F32),
        compiler_params=_params(1),
        name="mlp",
    )(*args)


def _gelu(z):
    return 0.5 * z * (1.0 + lax.erf(z * float(np.sqrt(0.5))))


SGU_COL_BLOCK = 2 * SGU_GROUP_DIM


def _sgu_kernel(x_ref, g_ref, shift_ref, scale_ref, gate_ref, win_ref, bin_ref, vg_ref, ws_ref,
                bs_ref, wout_ref, out_ref, v_scr, t_scr):
    tm = x_ref.shape[0]
    x = x_ref[...]
    h = _rms_mod(x, g_ref[...], shift_ref[...], scale_ref[...]).astype(BF16)
    zv = _gelu(jnp.dot(h, win_ref[:, SGU_DIM:], preferred_element_type=F32) + bin_ref[:, SGU_DIM:])
    r = lax.rsqrt(jnp.mean(zv * zv, axis=-1, keepdims=True) + EPS)
    v_scr[...] = (zv * r * vg_ref[...]).astype(BF16)
    for cb in range(SGU_DIM // SGU_COL_BLOCK):
        cols = slice(cb * SGU_COL_BLOCK, (cb + 1) * SGU_COL_BLOCK)
        u = _gelu(jnp.dot(h, win_ref[:, cols], preferred_element_type=F32) + bin_ref[:, cols])
        for gi in range(SGU_COL_BLOCK // SGU_GROUP_DIM):
            grp = cb * (SGU_COL_BLOCK // SGU_GROUP_DIM) + gi
            w_s = ws_ref[grp]
            b_s = bs_ref[grp]
            for c in range(tm // CHUNK):
                rows = slice(c * CHUNK, (c + 1) * CHUNK)
                gcols = slice(grp * SGU_GROUP_DIM, (grp + 1) * SGU_GROUP_DIM)
                sv = jnp.dot(w_s, v_scr[rows, gcols], preferred_element_type=F32) + b_s
                lcols = slice(gi * SGU_GROUP_DIM, (gi + 1) * SGU_GROUP_DIM)
                t_scr[rows, gcols] = (u[rows, lcols] * sv).astype(BF16)
    y = jnp.dot(t_scr[...], wout_ref[...], preferred_element_type=F32)
    out_ref[...] = x + gate_ref[...] * y


def _sgu(x2d, tm, mod, mod_tiles_per_row, norm_g, w_in, b_in, v_g, w_s, b_s, w_out):
    t = x2d.shape[0]
    const = lambda i: (0, 0)
    const3 = lambda i: (0, 0, 0)
    tile = pl.BlockSpec((tm, D_MODEL), lambda i: (i, 0))
    return pl.pallas_call(
        _sgu_kernel,
        grid=(t // tm,),
        in_specs=[
            tile,
            pl.BlockSpec((1, D_MODEL), const),
            _mod_spec(0, mod_tiles_per_row),
            _mod_spec(1, mod_tiles_per_row),
            _mod_spec(2, mod_tiles_per_row),
            _resident((D_MODEL, 2 * SGU_DIM), const),
            pl.BlockSpec((1, 2 * SGU_DIM), const),
            pl.BlockSpec((1, SGU_DIM), const),
            pl.BlockSpec((SGU_GROUPS, CHUNK, CHUNK), const3),
            pl.BlockSpec((SGU_GROUPS, CHUNK, 1), const3),
            _resident((SGU_DIM, D_MODEL), const),
        ],
        out_specs=tile,
        out_shape=jax.ShapeDtypeStruct(x2d.shape, F32),
        scratch_shapes=[pltpu.VMEM((tm, SGU_DIM), BF16), pltpu.VMEM((tm, SGU_DIM), BF16)],
        compiler_params=_params(1),
        name="sgu",
    )(x2d, norm_g, mod, mod, mod, w_in, b_in, v_g, w_s, b_s, w_out)


def _pairs_apart(w, n_heads):
    split = w.reshape(n_heads, HEAD_DIM // 2, 2, *w.shape[1:])
    return jnp.swapaxes(split, 1, 2).reshape(w.shape)


def kernel(x, c, ctx, c_ctx, ada_w, ada_b, mix_norm_g, mlp_norm_g, mlp_w1, mlp_w2, attn_wqkv,
           attn_q_g, attn_k_g, attn_wo, sgu_w_in, sgu_b_in, sgu_v_g, sgu_w_s, sgu_b_s, sgu_w_out,
           final_g):
    b, n, d = x.shape
    lc = ctx.shape[1]
    tm_x, tm_mlp, tm_sgu, tq, tk = QKV_TILE, MLP_TILE, SGU_TILE, FLASH_Q_TILE, FLASH_K_TILE
    tm_c = lc

    cond = jnp.concatenate([c, c_ctx[None, :], jnp.zeros((MOD_ROWS - b - 1, d), F32)], axis=0)
    mods = _adaln(cond, ada_w, ada_b)
    cos_t, sin_t = _rope_tables(n)
    cos_c = jnp.ones((HEAD_DIM // 2, lc), F32)
    sin_c = jnp.zeros((HEAD_DIM // 2, lc), F32)

    x2 = x.reshape(b * n, d)
    c2 = ctx.reshape(b * lc, d)
    row = lambda v: v.reshape(1, -1)
    stacked = dict(w1=mlp_w1, w2=mlp_w2, wo=attn_wo, w_in=sgu_w_in, w_out=sgu_w_out)
    bf16 = None
    for i in range(DEPTH):
        last = i == DEPTH - 1
        use_attn = (i % N_MIXERS) == 0
        j = i // N_MIXERS
        mod_x = mods[i, :b].reshape(b, 1, N_MOD * d)
        mod_c = mods[i, b:b + 1].reshape(1, 1, N_MOD * d)
        x_tpr, c_tpr = n // tm_x, (b * lc) // tm_c
        mix_g = row(mix_norm_g[i])
        mlp_g = row(mlp_norm_g[i])
        fg = row(final_g) if last else None
        if use_attn:
            w_t = attn_wqkv[j].T
            wqkv_t = jnp.concatenate(
                [_pairs_apart(w_t[:Q_DIM + KV_DIM], N_HEADS + N_KV_HEADS), w_t[Q_DIM + KV_DIM:]],
                axis=0).astype(BF16)
            q_g = _pairs_apart(attn_q_g[j], 1)[:, None]
            k_g = _pairs_apart(attn_k_g[j], 1)[:, None]
            q_t, k, vt = _qkv(x2, n, tm_x, mod_x, x_tpr, mix_g, wqkv_t, q_g, k_g, cos_t, sin_t)
            qc_t, kc, vtc = _qkv(c2, lc, tm_c, mod_c, c_tpr, mix_g, wqkv_t, q_g, k_g,
                                 cos_c, sin_c)
            to_cast = () if bf16 else tuple(w.reshape(-1, w.shape[-1]) for w in stacked.values())
            o, casted = _flash(q_t, [(k, vt), (kc, vtc)], n, tq, tk, cast=to_cast)
            if not bf16:
                bf16 = {name: wb.reshape(w.shape)
                        for (name, w), wb in zip(stacked.items(), casted)}
            w1, w2, wo = bf16["w1"], bf16["w2"], bf16["wo"][j]
            x2 = _mlp(x2, tm_mlp, mod_x, n // tm_mlp, mlp_g, w1, w2, i, attn_o=o, wo=wo,
                      final_g=fg)
            if not last:
                oc, _ = _flash(qc_t, [(kc, vtc)], lc, lc, tk)
                c2 = _mlp(c2, tm_c, mod_c, c_tpr, mlp_g, w1, w2, i, attn_o=oc, wo=wo)
        else:
            w1, w2 = bf16["w1"], bf16["w2"]
            sgu_args = (bf16["w_in"][j], row(sgu_b_in[j]), row(sgu_v_g[j]),
                        sgu_w_s[j].astype(BF16), sgu_b_s[j][:, :, None], bf16["w_out"][j])
            x2 = _sgu(x2, tm_sgu, mod_x, n // tm_sgu, mix_g, *sgu_args)
            x2 = _mlp(x2, tm_mlp, mod_x, n // tm_mlp, mlp_g, w1, w2, i, final_g=fg)
            if not last:
                c2 = _sgu(c2, tm_c, mod_c, c_tpr, mix_g, *sgu_args)
                c2 = _mlp(c2, tm_c, mod_c, c_tpr, mlp_g, w1, w2, i)
    return x2.reshape(b, n, d)
```

```python
import functools

import jax
import jax.numpy as jnp
import numpy as np
from jax import lax
from jax.experimental import pallas as pl
from jax.experimental.pallas import tpu as pltpu

D_MODEL = 1024
DEPTH = 2
GRID_W = 64
N_MIXERS = 2
N_HEADS = 8
N_KV_HEADS = 2
HEAD_DIM = 128
GROUP = N_HEADS // N_KV_HEADS
Q_DIM = N_HEADS * HEAD_DIM
KV_DIM = N_KV_HEADS * HEAD_DIM
QKV_DIM = Q_DIM + 2 * KV_DIM
ROPE_AXIS_DIM = HEAD_DIM // 2
ROPE_THETA = 10000.0
CHUNK = 128
SGU_DIM = 3 * D_MODEL
SGU_GROUPS = 8
SGU_GROUP_DIM = SGU_DIM // SGU_GROUPS
D_FF = 4 * D_MODEL
N_MOD = 6
EPS = 1e-6

F32 = jnp.float32
BF16 = jnp.bfloat16

VMEM_LIMIT_BYTES = 56 * 1024 * 1024
MOD_ROWS = 8

QKV_TILE = 1024
MLP_TILE = 1024
SGU_TILE = 4 * CHUNK
FLASH_Q_TILE = 1024
FLASH_K_TILE = 4096

SOFTMAX_Q_SCALE = float(HEAD_DIM ** -0.5 * np.log2(np.e))
NEG_BIG = -1e30
BF16_SUBLANE_TILE = 16


def _params(n_axes):
    return pltpu.CompilerParams(dimension_semantics=("parallel",) * n_axes,
                                vmem_limit_bytes=VMEM_LIMIT_BYTES)


def _resident(shape, index_map):
    return pl.BlockSpec(shape, index_map, pipeline_mode=pl.Buffered(1))


def _rms_mod(x, g, shift, scale):
    gain = g * (1.0 + scale)
    return x * lax.rsqrt(jnp.mean(x * x, axis=-1, keepdims=True) + EPS) * gain + shift


def _mod_spec(k, tiles_per_row):
    return pl.BlockSpec((None, 1, D_MODEL), lambda i: (i // tiles_per_row, 0, k))


def _adaln_kernel(c_ref, w_ref, b_ref, o_ref):
    c = c_ref[...]
    s = c * jax.nn.sigmoid(c)
    o_ref[...] = jnp.dot(s.astype(BF16), w_ref[...].astype(BF16),
                         preferred_element_type=F32) + b_ref[...]


def _adaln(cond, ada_w, ada_b):
    depth = ada_w.shape[0]
    return pl.pallas_call(
        _adaln_kernel,
        grid=(depth, N_MOD),
        in_specs=[
            pl.BlockSpec((MOD_ROWS, D_MODEL), lambda i, j: (0, 0)),
            pl.BlockSpec((None, D_MODEL, D_MODEL), lambda i, j: (i, 0, j)),
            pl.BlockSpec((None, 1, D_MODEL), lambda i, j: (i, 0, j)),
        ],
        out_specs=pl.BlockSpec((None, MOD_ROWS, D_MODEL), lambda i, j: (i, 0, j)),
        out_shape=jax.ShapeDtypeStruct((depth, MOD_ROWS, N_MOD * D_MODEL), F32),
        compiler_params=_params(2),
        name="adaln",
    )(cond, ada_w, ada_b.reshape(depth, 1, N_MOD * D_MODEL))


def _cos_sin_kernel(ang_ref, cos_ref, sin_ref):
    a = ang_ref[...]
    cos_ref[...] = jnp.cos(a)
    sin_ref[...] = jnp.sin(a)


def _rope_tables(n):
    rows_count = n // GRID_W
    freqs = 1.0 / (ROPE_THETA ** (jnp.arange(0, ROPE_AXIS_DIM, 2, dtype=F32) / ROPE_AXIS_DIM))
    pos = jnp.arange(rows_count + GRID_W, dtype=jnp.int32)
    pos = jnp.where(pos < rows_count, pos, pos - rows_count).astype(F32)
    ang = pos[:, None] * freqs[None, :]
    cos_s, sin_s = pl.pallas_call(
        _cos_sin_kernel,
        out_shape=(jax.ShapeDtypeStruct(ang.shape, F32),) * 2,
        name="rope_cos_sin",
    )(ang)

    def expand(t):
        by_row = jnp.repeat(t[:rows_count], GRID_W, axis=0)
        by_col = jnp.tile(t[rows_count:], (rows_count, 1))
        return jnp.concatenate([by_row, by_col], axis=-1)

    return expand(cos_s).T, expand(sin_s).T


Q_HEADS_PER_DOT = 4


def _qkv_kernel(x_ref, g_ref, shift_ref, scale_ref, wt_ref, qg_ref, kg_ref, cos_ref, sin_ref,
                qt_ref, k_ref, vt_ref):
    h =_rms_mod(x_ref[...], g_ref[...], shift_ref[...], scale_ref[...]).astype(BF16)
    cos = cos_ref[...]
    sin = sin_ref[...]
    half = HEAD_DIM // 2
    nt_dims = (((1,), (1,)), ((), ()))

    def norm_rope(t, g):
        t = t * lax.rsqrt(jnp.mean(t * t, axis=0, keepdims=True) + EPS) * g
        x1, x2 = t[:half], t[half:]
        return x1 * cos - x2 * sin, x1 * sin + x2 * cos

    qg = qg_ref[...]
    for blk in range(N_HEADS // Q_HEADS_PER_DOT):
        rows = slice(blk * Q_HEADS_PER_DOT * HEAD_DIM, (blk + 1) * Q_HEADS_PER_DOT * HEAD_DIM)
        yt = lax.dot_general(wt_ref[rows, :], h, nt_dims, preferred_element_type=F32)
        for j in range(Q_HEADS_PER_DOT):
            r0 = (blk * Q_HEADS_PER_DOT + j) * HEAD_DIM
            o1, o2 = norm_rope(yt[j * HEAD_DIM:(j + 1) * HEAD_DIM], qg)
            qt_ref[r0:r0 + half, :] = (o1 * SOFTMAX_Q_SCALE).astype(BF16)
            qt_ref[r0 + half:r0 + HEAD_DIM, :] = (o2 * SOFTMAX_Q_SCALE).astype(BF16)
    yt = lax.dot_general(wt_ref[Q_DIM:, :], h, nt_dims, preferred_element_type=F32)
    kg = kg_ref[...]
    for hd in range(N_KV_HEADS):
        o1, o2 = norm_rope(yt[hd * HEAD_DIM:(hd + 1) * HEAD_DIM], kg)
        k_ref[hd] = jnp.concatenate([o1, o2], axis=0).T.astype(BF16)
        v_t = yt[KV_DIM + hd * HEAD_DIM:KV_DIM + (hd + 1) * HEAD_DIM]
        vt_ref[hd] = v_t.astype(BF16)


def _qkv(x2d, seq, tm, mod, mod_tiles_per_row, norm_g, wqkv_t, q_g, k_g, cos_t, sin_t):
    t = x2d.shape[0]
    tpb = seq // tm
    const = lambda i: (0, 0)
    return pl.pallas_call(
        _qkv_kernel,
        grid=(t // tm,),
        in_specs=[
            pl.BlockSpec((tm, D_MODEL), lambda i: (i, 0)),
            pl.BlockSpec((1, D_MODEL), const),
            _mod_spec(0, mod_tiles_per_row),
            _mod_spec(1, mod_tiles_per_row),
            _resident((QKV_DIM, D_MODEL), const),
            pl.BlockSpec((HEAD_DIM, 1), const),
            pl.BlockSpec((HEAD_DIM, 1), const),
            pl.BlockSpec((HEAD_DIM // 2, tm), lambda i: (0, i % tpb)),
            pl.BlockSpec((HEAD_DIM // 2, tm), lambda i: (0, i % tpb)),
        ],
        out_specs=[
            pl.BlockSpec((Q_DIM, tm), lambda i: (0, i)),
            pl.BlockSpec((None, N_KV_HEADS, tm, HEAD_DIM), lambda i: (i // tpb, 0, i % tpb, 0)),
            pl.BlockSpec((None, N_KV_HEADS, HEAD_DIM, tm), lambda i: (i // tpb, 0, 0, i % tpb)),
        ],
        out_shape=[
            jax.ShapeDtypeStruct((Q_DIM, t), BF16),
            jax.ShapeDtypeStruct((t // seq, N_KV_HEADS, seq, HEAD_DIM), BF16),
            jax.ShapeDtypeStruct((t // seq, N_KV_HEADS, HEAD_DIM, seq), BF16),
        ],
        compiler_params=_params(1),
        name="qkv_proj",
    )(x2d, norm_g, mod, mod, wqkv_t, q_g, k_g, cos_t, sin_t)


MAX_TILES_PER_HEAD = 16
L_MIN = 2.0 ** -60


def _flash_kernel(*refs, tiles, n_cast):
    n_src = len(tiles)
    n_in = 1 + 2 * n_src + n_cast
    q_ref = refs[0]
    k_refs = refs[1:1 + 2 * n_src:2]
    vt_refs = refs[2:2 + 2 * n_src:2]
    o_ref = refs[n_in]
    shift_scr, acc_scr, l_scr, kmax_scr = refs[n_in + 1 + n_cast:]

    for w_ref, w_bf16_ref in zip(refs[n_in - n_cast:n_in], refs[n_in + 1:n_in + 1 + n_cast]):
        w_bf16_ref[...] = w_ref[...].astype(BF16)

    @pl.when(pl.program_id(2) == 0)
    def _():
        kmax2 = jnp.zeros((1, 1), F32)
        for k_ref, (tk, nk) in zip(k_refs, tiles):
            def body(t, n2, k_ref=k_ref, tk=tk):
                kk = k_ref[pl.ds(pl.multiple_of(t * tk, tk), tk), :].astype(F32)
                return jnp.maximum(n2, jnp.sum(kk * kk, axis=1, keepdims=True))

            n2 = lax.fori_loop(0, nk, body, jnp.zeros((tk, 1), F32))
            kmax2 = jnp.maximum(kmax2, jnp.max(n2, axis=0, keepdims=True))
        kmax_scr[...] = jnp.broadcast_to(jnp.sqrt(kmax2), kmax_scr.shape)

    def operands(src, g, t):
        tk = tiles[src][0]
        ks = t * tk if isinstance(t, int) else pl.multiple_of(t * tk, tk)
        qs = g * HEAD_DIM if isinstance(g, int) else pl.multiple_of(g * HEAD_DIM, HEAD_DIM)
        q_t = q_ref[pl.ds(qs, HEAD_DIM), :]
        return k_refs[src][pl.ds(ks, tk), :], q_t, vt_refs[src][:, pl.ds(ks, tk)]

    def finalize():
        l_min = None
        for g in range(GROUP):
            l = l_scr[g]
            o = acc_scr[g] * (1.0 / l)
            o_ref[:, g * HEAD_DIM:(g + 1) * HEAD_DIM] = o.astype(BF16).T
            l_min = l if l_min is None else jnp.minimum(l_min, l)
        return jnp.min(l_min)

    for g in range(GROUP):
        qf = q_ref[g * HEAD_DIM:(g + 1) * HEAD_DIM, :].astype(F32)
        shift_scr[g] = jnp.sqrt(jnp.sum(qf * qf, axis=0, keepdims=True)) * kmax_scr[...]

    def bounded_step(src, g, t):
        k_tile, q_t, vt_tile = operands(src, g, t)
        s = jnp.dot(k_tile, q_t, preferred_element_type=F32)
        p = jnp.exp2(s - shift_scr[g])
        l_tile = jnp.sum(p, axis=0, keepdims=True)
        pv = jnp.dot(vt_tile, p.astype(BF16), preferred_element_type=F32)
        if (src, t) == (0, 0):
            acc_scr[g], l_scr[g] = pv, l_tile
        else:
            acc_scr[g] += pv
            l_scr[g] += l_tile

    def per_head(step_fn):
        def body(g, carry):
            for src, (_, nk) in enumerate(tiles):
                for t in range(nk):
                    step_fn(src, g, t)
            return carry

        lax.fori_loop(0, GROUP, body, 0)

    per_head(bounded_step)
    l_min = finalize()

    @pl.when(jnp.logical_not(l_min >= L_MIN))
    def _():
        shift_scr[...] = jnp.full(shift_scr.shape, NEG_BIG, F32)
        acc_scr[...] = jnp.zeros(acc_scr.shape, F32)
        l_scr[...] = jnp.zeros(l_scr.shape, F32)

        def online_step(src, g, t):
            k_tile, q_t, vt_tile = operands(src, g, t)
            s = jnp.dot(k_tile, q_t, preferred_element_type=F32)
            m_old = shift_scr[g]
            m_new = jnp.maximum(m_old, jnp.max(s, axis=0, keepdims=True))
            shift_scr[g] = m_new
            alpha = jnp.exp2(m_old - m_new)
            p = jnp.exp2(s - m_new)
            l_scr[g] = alpha * l_scr[g] + jnp.sum(p, axis=0, keepdims=True)
            pv = jnp.dot(vt_tile, p.astype(BF16), preferred_element_type=F32)
            acc_scr[g] = alpha * acc_scr[g] + pv

        for src, (_, nk) in enumerate(tiles):
            def body(u, carry, src=src, nk=nk):
                g = u // nk
                online_step(src, g, u - g * nk)
                return carry

            lax.fori_loop(0, GROUP * nk, body, 0)
        finalize()


def _flash(q_t, sources, seq_q, tq, tk, cast=()):
    b = sources[0][0].shape[0]
    nq = seq_q // tq
    gw = GROUP * HEAD_DIM
    n_grid = b * N_KV_HEADS * nq
    grid_pos = lambda bi, h, i: ((bi * N_KV_HEADS + h) * nq + i, 0)
    cast_specs = []
    for w in cast:
        rows, rem = divmod(w.shape[0], n_grid)
        assert rem == 0 and rows % BF16_SUBLANE_TILE == 0, w.shape
        cast_specs.append(pl.BlockSpec((rows, w.shape[1]), grid_pos))
    kv_args, kv_specs, tiles = [], [], []
    for k, vt in sources:
        lk = k.shape[2]
        rows = min(tk, lk)
        tiles.append((rows, lk // rows))
        kv_args += [k, vt]
        assert sum(nk for _, nk in tiles) <= MAX_TILES_PER_HEAD, tiles
        kv_specs += [pl.BlockSpec((None, None, lk, HEAD_DIM), lambda bi, h, i: (bi, h, 0, 0)),
                     pl.BlockSpec((None, None, HEAD_DIM, lk), lambda bi, h, i: (bi, h, 0, 0))]
    out = pl.pallas_call(
        functools.partial(_flash_kernel, tiles=tuple(tiles), n_cast=len(cast)),
        grid=(b, N_KV_HEADS, nq),
        in_specs=([pl.BlockSpec((gw, tq), lambda bi, h, i: (h, bi * nq + i))] + kv_specs
                  + cast_specs),
        out_specs=[pl.BlockSpec((tq, gw), lambda bi, h, i: (bi * nq + i, h))] + cast_specs,
        out_shape=([jax.ShapeDtypeStruct((q_t.shape[1], q_t.shape[0]), BF16)]
                   + [jax.ShapeDtypeStruct(w.shape, BF16) for w in cast]),
        scratch_shapes=[pltpu.VMEM((GROUP, 1, tq), F32), pltpu.VMEM((GROUP, HEAD_DIM, tq), F32),
                        pltpu.VMEM((GROUP, 1, tq), F32), pltpu.VMEM((1, tq), F32)],
        compiler_params=pltpu.CompilerParams(
            dimension_semantics=("parallel", "parallel", "arbitrary"),
            vmem_limit_bytes=VMEM_LIMIT_BYTES),
        name="flash_attn",
    )(q_t, *kv_args, *cast)
    return out[0], out[1:]


def _mlp_kernel(*refs, with_proj, with_final_norm, ff_chunk):
    refs = list(refs)
    x_ref = refs.pop(0)
    if with_proj:
        o_ref, wo_ref, gate_mix_ref = refs.pop(0), refs.pop(0), refs.pop(0)
    g_ref, shift_ref, scale_ref, gate_ref, w1_ref, w2_ref = refs[:6]
    refs = refs[6:]
    if with_final_norm:
        fg_ref = refs.pop(0)
    (out_ref,) = refs

    x = x_ref[...]
    if with_proj:
        x = x + gate_mix_ref[...] * jnp.dot(o_ref[...], wo_ref[...], preferred_element_type=F32)
    h = _rms_mod(x, g_ref[...], shift_ref[...], scale_ref[...]).astype(BF16)
    y = jnp.zeros_like(x)
    for c in range(D_FF // ff_chunk):
        sl = slice(c * ff_chunk, (c + 1) * ff_chunk)
        a = jnp.maximum(jnp.dot(h, w1_ref[:, sl], preferred_element_type=F32), 0.0)
        y = y + jnp.dot((a * a).astype(BF16), w2_ref[sl, :], preferred_element_type=F32)
    x = x + gate_ref[...] * y
    if with_final_norm:
        x = x * lax.rsqrt(jnp.mean(x * x, axis=-1, keepdims=True) + EPS) * fg_ref[...]
    out_ref[...] = x


def _mlp(x2d, tm, mod, mod_tiles_per_row, norm_g, w1, w2, layer, *, attn_o=None, wo=None,
         final_g=None):
    t = x2d.shape[0]
    const = lambda i: (0, 0)
    this_layer = lambda i: (layer, 0, 0)
    tile = pl.BlockSpec((tm, D_MODEL), lambda i: (i, 0))
    args, specs = [x2d], [tile]
    if attn_o is not None:
        args += [attn_o, wo, mod]
        specs += [pl.BlockSpec((tm, Q_DIM), lambda i: (i, 0)), _resident((Q_DIM, D_MODEL), const),
                  _mod_spec(2, mod_tiles_per_row)]
    args += [norm_g, mod, mod, mod, w1, w2]
    specs += [pl.BlockSpec((1, D_MODEL), const), _mod_spec(3, mod_tiles_per_row),
              _mod_spec(4, mod_tiles_per_row), _mod_spec(5, mod_tiles_per_row),
              _resident((None, D_MODEL, D_FF), this_layer),
              _resident((None, D_FF, D_MODEL), this_layer)]
    if final_g is not None:
        args.append(final_g)
        specs.append(pl.BlockSpec((1, D_MODEL), const))
    return pl.pallas_call(
        functools.partial(_mlp_kernel, with_proj=attn_o is not None,
                          with_final_norm=final_g is not None, ff_chunk=1024),
        grid=(t // tm,),
        in_specs=specs,
        out_specs=tile,
        out_shape=jax.ShapeDtypeStruct(x2d.shape, F32),
        compiler_params=_params(1),
        name="mlp",
    )(*args)


def _gelu(z):
    return 0.5 * z * (1.0 + lax.erf(z * float(np.sqrt(0.5))))


SGU_COL_BLOCK = 2 * SGU_GROUP_DIM


def _sgu_kernel(x_ref, g_ref, shift_ref, scale_ref, gate_ref, win_ref, bin_ref, vg_ref, ws_ref,
                bs_ref, wout_ref, out_ref, v_scr, t_scr):
    tm = x_ref.shape[0]
    x = x_ref[...]
    h = _rms_mod(x, g_ref[...], shift_ref[...], scale_ref[...]).astype(BF16)
    zv = _gelu(jnp.dot(h, win_ref[:, SGU_DIM:], preferred_element_type=F32) + bin_ref[:, SGU_DIM:])
    r = lax.rsqrt(jnp.mean(zv * zv, axis=-1, keepdims=True) + EPS)
    v_scr[...] = (zv * r * vg_ref[...]).astype(BF16)
    for cb in range(SGU_DIM // SGU_COL_BLOCK):
        cols = slice(cb * SGU_COL_BLOCK, (cb + 1) * SGU_COL_BLOCK)
        u = _gelu(jnp.dot(h, win_ref[:, cols], preferred_element_type=F32) + bin_ref[:, cols])
        for gi in range(SGU_COL_BLOCK // SGU_GROUP_DIM):
            grp = cb * (SGU_COL_BLOCK // SGU_GROUP_DIM) + gi
            w_s = ws_ref[grp]
            b_s = bs_ref[grp]
            for c in range(tm // CHUNK):
                rows = slice(c * CHUNK, (c + 1) * CHUNK)
                gcols = slice(grp * SGU_GROUP_DIM, (grp + 1) * SGU_GROUP_DIM)
                sv = jnp.dot(w_s, v_scr[rows, gcols], preferred_element_type=F32) + b_s
                lcols = slice(gi * SGU_GROUP_DIM, (gi + 1) * SGU_GROUP_DIM)
                t_scr[rows, gcols] = (u[rows, lcols] * sv).astype(BF16)
    y = jnp.dot(t_scr[...], wout_ref[...], preferred_element_type=F32)
    out_ref[...] = x + gate_ref[...] * y


def _sgu(x2d, tm, mod, mod_tiles_per_row, norm_g, w_in, b_in, v_g, w_s, b_s, w_out):
    t = x2d.shape[0]
    const = lambda i: (0, 0)
    const3 = lambda i: (0, 0, 0)
    tile = pl.BlockSpec((tm, D_MODEL), lambda i: (i, 0))
    return pl.pallas_call(
        _sgu_kernel,
        grid=(t // tm,),
        in_specs=[
            tile,
            pl.BlockSpec((1, D_MODEL), const),
            _mod_spec(0, mod_tiles_per_row),
            _mod_spec(1, mod_tiles_per_row),
            _mod_spec(2, mod_tiles_per_row),
            _resident((D_MODEL, 2 * SGU_DIM), const),
            pl.BlockSpec((1, 2 * SGU_DIM), const),
            pl.BlockSpec((1, SGU_DIM), const),
            pl.BlockSpec((SGU_GROUPS, CHUNK, CHUNK), const3),
            pl.BlockSpec((SGU_GROUPS, CHUNK, 1), const3),
            _resident((SGU_DIM, D_MODEL), const),
        ],
        out_specs=tile,
        out_shape=jax.ShapeDtypeStruct(x2d.shape, F32),
        scratch_shapes=[pltpu.VMEM((tm, SGU_DIM), BF16), pltpu.VMEM((tm, SGU_DIM), BF16)],
        compiler_params=_params(1),
        name="sgu",
    )(x2d, norm_g, mod, mod, mod, w_in, b_in, v_g, w_s, b_s, w_out)


def _pairs_apart(w, n_heads):
    split = w.reshape(n_heads, HEAD_DIM // 2, 2, *w.shape[1:])
    return jnp.swapaxes(split, 1, 2).reshape(w.shape)


def kernel(x, c, ctx, c_ctx, ada_w, ada_b, mix_norm_g, mlp_norm_g, mlp_w1, mlp_w2, attn_wqkv,
           attn_q_g, attn_k_g, attn_wo, sgu_w_in, sgu_b_in, sgu_v_g, sgu_w_s, sgu_b_s, sgu_w_out,
           final_g):
    b, n, d = x.shape
    lc = ctx.shape[1]
    tm_x, tm_mlp, tm_sgu, tq, tk = QKV_TILE, MLP_TILE, SGU_TILE, FLASH_Q_TILE, FLASH_K_TILE
    tm_c = lc

    cond = jnp.concatenate([c, c_ctx[None, :], jnp.zeros((MOD_ROWS - b - 1, d), F32)], axis=0)
    mods = _adaln(cond, ada_w, ada_b)
    cos_t, sin_t = _rope_tables(n)
    cos_c = jnp.ones((HEAD_DIM // 2, lc), F32)
    sin_c = jnp.zeros((HEAD_DIM // 2, lc), F32)

    x2 = x.reshape(b * n, d)
    c2 = ctx.reshape(b * lc, d)
    row = lambda v: v.reshape(1, -1)
    stacked = dict(w1=mlp_w1, w2=mlp_w2, wo=attn_wo, w_in=sgu_w_in, w_out=sgu_w_out)
    bf16 = None
    for i in range(DEPTH):
        last = i == DEPTH - 1
        use_attn = (i % N_MIXERS) == 0
        j = i // N_MIXERS
        mod_x = mods[i, :b].reshape(b, 1, N_MOD * d)
        mod_c = mods[i, b:b + 1].reshape(1, 1, N_MOD * d)
        x_tpr, c_tpr = n // tm_x, (b * lc) // tm_c
        mix_g = row(mix_norm_g[i])
        mlp_g = row(mlp_norm_g[i])
        fg = row(final_g) if last else None
        if use_attn:
            w_t = attn_wqkv[j].T
            wqkv_t = jnp.concatenate(
                [_pairs_apart(w_t[:Q_DIM + KV_DIM], N_HEADS + N_KV_HEADS), w_t[Q_DIM + KV_DIM:]],
                axis=0).astype(BF16)
            q_g = _pairs_apart(attn_q_g[j], 1)[:, None]
            k_g = _pairs_apart(attn_k_g[j], 1)[:, None]
            q_t, k, vt = _qkv(x2, n, tm_x, mod_x, x_tpr, mix_g, wqkv_t, q_g, k_g, cos_t, sin_t)
            qc_t, kc, vtc = _qkv(c2, lc, tm_c, mod_c, c_tpr, mix_g, wqkv_t, q_g, k_g,
                                 cos_c, sin_c)
            to_cast = () if bf16 else tuple(w.reshape(-1, w.shape[-1]) for w in stacked.values())
            o, casted = _flash(q_t, [(k, vt), (kc, vtc)], n, tq, tk, cast=to_cast)
            if not bf16:
                bf16 = {name: wb.reshape(w.shape)
                        for (name, w), wb in zip(stacked.items(), casted)}
            w1, w2, wo = bf16["w1"], bf16["w2"], bf16["wo"][j]
            x2 = _mlp(x2, tm_mlp, mod_x, n // tm_mlp, mlp_g, w1, w2, i, attn_o=o, wo=wo,
                      final_g=fg)
            if not last:
                oc, _ = _flash(qc_t, [(kc, vtc)], lc, lc, tk)
                c2 = _mlp(c2, tm_c, mod_c, c_tpr, mlp_g, w1, w2, i, attn_o=oc, wo=wo)
        else:
            w1, w2 = bf16["w1"], bf16["w2"]
            sgu_args = (bf16["w_in"][j], row(sgu_b_in[j]), row(sgu_v_g[j]),
                        sgu_w_s[j].astype(BF16), sgu_b_s[j][:, :, None], bf16["w_out"][j])
            x2 = _sgu(x2, tm_sgu, mod_x, n // tm_sgu, mix_g, *sgu_args)
            x2 = _mlp(x2, tm_mlp, mod_x, n // tm_mlp, mlp_g, w1, w2, i, final_g=fg)
            if not last:
                c2 = _sgu(c2, tm_c, mod_c, c_tpr, mix_g, *sgu_args)
                c2 = _mlp(c2, tm_c, mod_c, c_tpr, mlp_g, w1, w2, i)
    return x2.reshape(b, n, d)
```

```python
import functools

import jax
import jax.numpy as jnp
import numpy as np
from jax import lax
from jax.experimental import pallas as pl
from jax.experimental.pallas import tpu as pltpu

D_MODEL = 1024
DEPTH = 2
GRID_W = 64
N_MIXERS = 2
N_HEADS = 8
N_KV_HEADS = 2
HEAD_DIM = 128
GROUP = N_HEADS // N_KV_HEADS
Q_DIM = N_HEADS * HEAD_DIM
KV_DIM = N_KV_HEADS * HEAD_DIM
QKV_DIM = Q_DIM + 2 * KV_DIM
ROPE_AXIS_DIM = HEAD_DIM // 2
ROPE_THETA = 10000.0
CHUNK = 128
SGU_DIM = 3 * D_MODEL
SGU_GROUPS = 8
SGU_GROUP_DIM = SGU_DIM // SGU_GROUPS
D_FF = 4 * D_MODEL
N_MOD = 6
EPS = 1e-6

F32 = jnp.float32
BF16 = jnp.bfloat16

VMEM_LIMIT_BYTES = 56 * 1024 * 1024
MOD_ROWS = 8

QKV_TILE = 1024
MLP_TILE = 1024
SGU_TILE = 4 * CHUNK
FLASH_Q_TILE = 1024
FLASH_K_TILE = 4096

SOFTMAX_Q_SCALE = float(HEAD_DIM ** -0.5 * np.log2(np.e))
NEG_BIG = -1e30
BF16_SUBLANE_TILE = 16


def _params(n_axes):
    return pltpu.CompilerParams(dimension_semantics=("parallel",) * n_axes,
                                vmem_limit_bytes=VMEM_LIMIT_BYTES)


def _resident(shape, index_map):
    return pl.BlockSpec(shape, index_map, pipeline_mode=pl.Buffered(1))


def _rms_mod(x, g, shift, scale):
    gain = g * (1.0 + scale)
    return x * lax.rsqrt(jnp.mean(x * x, axis=-1, keepdims=True) + EPS) * gain + shift


def _mod_spec(k, tiles_per_row):
    return pl.BlockSpec((None, 1, D_MODEL), lambda i: (i // tiles_per_row, 0, k))


def _adaln_kernel(c_ref, w_ref, b_ref, o_ref):
    c = c_ref[...]
    s = c * jax.nn.sigmoid(c)
    o_ref[...] = jnp.dot(s.astype(BF16), w_ref[...].astype(BF16),
                         preferred_element_type=F32) + b_ref[...]


def _adaln(cond, ada_w, ada_b):
    depth = ada_w.shape[0]
    return pl.pallas_call(
        _adaln_kernel,
        grid=(depth, N_MOD),
        in_specs=[
            pl.BlockSpec((MOD_ROWS, D_MODEL), lambda i, j: (0, 0)),
            pl.BlockSpec((None, D_MODEL, D_MODEL), lambda i, j: (i, 0, j)),
            pl.BlockSpec((None, 1, D_MODEL), lambda i, j: (i, 0, j)),
        ],
        out_specs=pl.BlockSpec((None, MOD_ROWS, D_MODEL), lambda i, j: (i, 0, j)),
        out_shape=jax.ShapeDtypeStruct((depth, MOD_ROWS, N_MOD * D_MODEL), F32),
        compiler_params=_params(2),
        name="adaln",
    )(cond, ada_w, ada_b.reshape(depth, 1, N_MOD * D_MODEL))


def _cos_sin_kernel(ang_ref, cos_ref, sin_ref):
    a = ang_ref[...]
    cos_ref[...] = jnp.cos(a)
    sin_ref[...] = jnp.sin(a)


def _rope_tables(n):
    rows_count = n // GRID_W
    freqs = 1.0 / (ROPE_THETA ** (jnp.arange(0, ROPE_AXIS_DIM, 2, dtype=F32) / ROPE_AXIS_DIM))
    pos = jnp.arange(rows_count + GRID_W, dtype=jnp.int32)
    pos = jnp.where(pos < rows_count, pos, pos - rows_count).astype(F32)
    ang = pos[:, None] * freqs[None, :]
    cos_s, sin_s = pl.pallas_call(
        _cos_sin_kernel,
        out_shape=(jax.ShapeDtypeStruct(ang.shape, F32),) * 2,
        name="rope_cos_sin",
    )(ang)

    def expand(t):
        by_row = jnp.repeat(t[:rows_count], GRID_W, axis=0)
        by_col = jnp.tile(t[rows_count:], (rows_count, 1))
        return jnp.concatenate([by_row, by_col], axis=-1)

    return expand(cos_s).T, expand(sin_s).T


Q_HEADS_PER_DOT = 4


def _qkv_kernel(x_ref, g_ref, shift_ref, scale_ref, wt_ref, qg_ref, kg_ref, cos_ref, sin_ref,
                qt_ref, k_ref, vt_ref):
    h =_rms_mod(x_ref[...], g_ref[...], shift_ref[...], scale_ref[...]).astype(BF16)
    cos = cos_ref[...]
    sin = sin_ref[...]
    half = HEAD_DIM // 2
    nt_dims = (((1,), (1,)), ((), ()))

    def norm_rope(t, g):
        t = t * lax.rsqrt(jnp.mean(t * t, axis=0, keepdims=True) + EPS) * g
        x1, x2 = t[:half], t[half:]
        return x1 * cos - x2 * sin, x1 * sin + x2 * cos

    qg = qg_ref[...]
    for blk in range(N_HEADS // Q_HEADS_PER_DOT):
        rows = slice(blk * Q_HEADS_PER_DOT * HEAD_DIM, (blk + 1) * Q_HEADS_PER_DOT * HEAD_DIM)
        yt = lax.dot_general(wt_ref[rows, :], h, nt_dims, preferred_element_type=F32)
        for j in range(Q_HEADS_PER_DOT):
            r0 = (blk * Q_HEADS_PER_DOT + j) * HEAD_DIM
            o1, o2 = norm_rope(yt[j * HEAD_DIM:(j + 1) * HEAD_DIM], qg)
            qt_ref[r0:r0 + half, :] = (o1 * SOFTMAX_Q_SCALE).astype(BF16)
            qt_ref[r0 + half:r0 + HEAD_DIM, :] = (o2 * SOFTMAX_Q_SCALE).astype(BF16)
    yt = lax.dot_general(wt_ref[Q_DIM:, :], h, nt_dims, preferred_element_type=F32)
    kg = kg_ref[...]
    for hd in range(N_KV_HEADS):
        o1, o2 = norm_rope(yt[hd * HEAD_DIM:(hd + 1) * HEAD_DIM], kg)
        k_ref[hd] = jnp.concatenate([o1, o2], axis=0).T.astype(BF16)
        v_t = yt[KV_DIM + hd * HEAD_DIM:KV_DIM + (hd + 1) * HEAD_DIM]
        vt_ref[hd] = v_t.astype(BF16)


def _qkv(x2d, seq, tm, mod, mod_tiles_per_row, norm_g, wqkv_t, q_g, k_g, cos_t, sin_t):
    t = x2d.shape[0]
    tpb = seq // tm
    const = lambda i: (0, 0)
    return pl.pallas_call(
        _qkv_kernel,
        grid=(t // tm,),
        in_specs=[
            pl.BlockSpec((tm, D_MODEL), lambda i: (i, 0)),
            pl.BlockSpec((1, D_MODEL), const),
            _mod_spec(0, mod_tiles_per_row),
            _mod_spec(1, mod_tiles_per_row),
            _resident((QKV_DIM, D_MODEL), const),
            pl.BlockSpec((HEAD_DIM, 1), const),
            pl.BlockSpec((HEAD_DIM, 1), const),
            pl.BlockSpec((HEAD_DIM // 2, tm), lambda i: (0, i % tpb)),
            pl.BlockSpec((HEAD_DIM // 2, tm), lambda i: (0, i % tpb)),
        ],
        out_specs=[
            pl.BlockSpec((Q_DIM, tm), lambda i: (0, i)),
            pl.BlockSpec((None, N_KV_HEADS, tm, HEAD_DIM), lambda i: (i // tpb, 0, i % tpb, 0)),
            pl.BlockSpec((None, N_KV_HEADS, HEAD_DIM, tm), lambda i: (i // tpb, 0, 0, i % tpb)),
        ],
        out_shape=[
            jax.ShapeDtypeStruct((Q_DIM, t), BF16),
            jax.ShapeDtypeStruct((t // seq, N_KV_HEADS, seq, HEAD_DIM), BF16),
            jax.ShapeDtypeStruct((t // seq, N_KV_HEADS, HEAD_DIM, seq), BF16),
        ],
        compiler_params=_params(1),
        name="qkv_proj",
    )(x2d, norm_g, mod, mod, wqkv_t, q_g, k_g, cos_t, sin_t)


MAX_TILES_PER_HEAD = 16
L_MIN = 2.0 ** -60


def _flash_kernel(*refs, tiles, n_cast):
    n_src = len(tiles)
    n_in = 1 + 2 * n_src + n_cast
    q_ref = refs[0]
    k_refs = refs[1:1 + 2 * n_src:2]
    vt_refs = refs[2:2 + 2 * n_src:2]
    o_ref = refs[n_in]
    shift_scr, acc_scr, l_scr, kmax_scr = refs[n_in + 1 + n_cast:]

    for w_ref, w_bf16_ref in zip(refs[n_in - n_cast:n_in], refs[n_in + 1:n_in + 1 + n_cast]):
        w_bf16_ref[...] = w_ref[...].astype(BF16)

    @pl.when(pl.program_id(2) == 0)
    def _():
        kmax2 = jnp.zeros((1, 1), F32)
        for k_ref, (tk, nk) in zip(k_refs, tiles):
            def body(t, n2, k_ref=k_ref, tk=tk):
                kk = k_ref[pl.ds(pl.multiple_of(t * tk, tk), tk), :].astype(F32)
                return jnp.maximum(n2, jnp.sum(kk * kk, axis=1, keepdims=True))

            n2 = lax.fori_loop(0, nk, body, jnp.zeros((tk, 1), F32))
            kmax2 = jnp.maximum(kmax2, jnp.max(n2, axis=0, keepdims=True))
        kmax_scr[...] = jnp.broadcast_to(jnp.sqrt(kmax2), kmax_scr.shape)

    def operands(src, g, t):
        tk = tiles[src][0]
        ks = t * tk if isinstance(t, int) else pl.multiple_of(t * tk, tk)
        qs = g * HEAD_DIM if isinstance(g, int) else pl.multiple_of(g * HEAD_DIM, HEAD_DIM)
        q_t = q_ref[pl.ds(qs, HEAD_DIM), :]
        return k_refs[src][pl.ds(ks, tk), :], q_t, vt_refs[src][:, pl.ds(ks, tk)]

    def finalize():
        l_min = None
        for g in range(GROUP):
            l = l_scr[g]
            o = acc_scr[g] * (1.0 / l)
            o_ref[:, g * HEAD_DIM:(g + 1) * HEAD_DIM] = o.astype(BF16).T
            l_min = l if l_min is None else jnp.minimum(l_min, l)
        return jnp.min(l_min)

    for g in range(GROUP):
        qf = q_ref[g * HEAD_DIM:(g + 1) * HEAD_DIM, :].astype(F32)
        shift_scr[g] = jnp.sqrt(jnp.sum(qf * qf, axis=0, keepdims=True)) * kmax_scr[...]

    def bounded_step(src, g, t):
        k_tile, q_t, vt_tile = operands(src, g, t)
        s = jnp.dot(k_tile, q_t, preferred_element_type=F32)
        p = jnp.exp2(s - shift_scr[g])
        l_tile = jnp.sum(p, axis=0, keepdims=True)
        pv = jnp.dot(vt_tile, p.astype(BF16), preferred_element_type=F32)
        if (src, t) == (0, 0):
            acc_scr[g], l_scr[g] = pv, l_tile
        else:
            acc_scr[g] += pv
            l_scr[g] += l_tile

    def per_head(step_fn):
        def body(g, carry):
            for src, (_, nk) in enumerate(tiles):
                for t in range(nk):
                    step_fn(src, g, t)
            return carry

        lax.fori_loop(0, GROUP, body, 0)

    per_head(bounded_step)
    l_min = finalize()

    @pl.when(jnp.logical_not(l_min >= L_MIN))
    def _():
        shift_scr[...] = jnp.full(shift_scr.shape, NEG_BIG, F32)
        acc_scr[...] = jnp.zeros(acc_scr.shape, F32)
        l_scr[...] = jnp.zeros(l_scr.shape, F32)

        def online_step(src, g, t):
            k_tile, q_t, vt_tile = operands(src, g, t)
            s = jnp.dot(k_tile, q_t, preferred_element_type=F32)
            m_old = shift_scr[g]
            m_new = jnp.maximum(m_old, jnp.max(s, axis=0, keepdims=True))
            shift_scr[g] = m_new
            alpha = jnp.exp2(m_old - m_new)
            p = jnp.exp2(s - m_new)
            l_scr[g] = alpha * l_scr[g] + jnp.sum(p, axis=0, keepdims=True)
            pv = jnp.dot(vt_tile, p.astype(BF16), preferred_element_type=F32)
            acc_scr[g] = alpha * acc_scr[g] + pv

        for src, (_, nk) in enumerate(tiles):
            def body(u, carry, src=src, nk=nk):
                g = u // nk
                online_step(src, g, u - g * nk)
                return carry

            lax.fori_loop(0, GROUP * nk, body, 0)
        finalize()


def _flash(q_t, sources, seq_q, tq, tk, cast=()):
    b = sources[0][0].shape[0]
    nq = seq_q // tq
    gw = GROUP * HEAD_DIM
    n_grid = b * N_KV_HEADS * nq
    grid_pos = lambda bi, h, i: ((bi * N_KV_HEADS + h) * nq + i, 0)
    cast_specs = []
    for w in cast:
        rows, rem = divmod(w.shape[0], n_grid)
        assert rem == 0 and rows % BF16_SUBLANE_TILE == 0, w.shape
        cast_specs.append(pl.BlockSpec((rows, w.shape[1]), grid_pos))
    kv_args, kv_specs, tiles = [], [], []
    for k, vt in sources:
        lk = k.shape[2]
        rows = min(tk, lk)
        tiles.append((rows, lk // rows))
        kv_args += [k, vt]
        assert sum(nk for _, nk in tiles) <= MAX_TILES_PER_HEAD, tiles
        kv_specs += [pl.BlockSpec((None, None, lk, HEAD_DIM), lambda bi, h, i: (bi, h, 0, 0)),
                     pl.BlockSpec((None, None, HEAD_DIM, lk), lambda bi, h, i: (bi, h, 0, 0))]
    out = pl.pallas_call(
        functools.partial(_flash_kernel, tiles=tuple(tiles), n_cast=len(cast)),
        grid=(b, N_KV_HEADS, nq),
        in_specs=([pl.BlockSpec((gw, tq), lambda bi, h, i: (h, bi * nq + i))] + kv_specs
                  + cast_specs),
        out_specs=[pl.BlockSpec((tq, gw), lambda bi, h, i: (bi * nq + i, h))] + cast_specs,
        out_shape=([jax.ShapeDtypeStruct((q_t.shape[1], q_t.shape[0]), BF16)]
                   + [jax.ShapeDtypeStruct(w.shape, BF16) for w in cast]),
        scratch_shapes=[pltpu.VMEM((GROUP, 1, tq), F32), pltpu.VMEM((GROUP, HEAD_DIM, tq), F32),
                        pltpu.VMEM((GROUP, 1, tq), F32), pltpu.VMEM((1, tq), F32)],
        compiler_params=pltpu.CompilerParams(
            dimension_semantics=("parallel", "parallel", "arbitrary"),
            vmem_limit_bytes=VMEM_LIMIT_BYTES),
        name="flash_attn",
    )(q_t, *kv_args, *cast)
    return out[0], out[1:]


def _mlp_kernel(*refs, with_proj, with_final_norm, ff_chunk):
    refs = list(refs)
    x_ref = refs.pop(0)
    if with_proj:
        o_ref, wo_ref, gate_mix_ref = refs.pop(0), refs.pop(0), refs.pop(0)
    g_ref, shift_ref, scale_ref, gate_ref, w1_ref, w2_ref = refs[:6]
    refs = refs[6:]
    if with_final_norm:
        fg_ref = refs.pop(0)
    (out_ref,) = refs

    x = x_ref[...]
    if with_proj:
        x = x + gate_mix_ref[...] * jnp.dot(o_ref[...], wo_ref[...], preferred_element_type=F32)
    h = _rms_mod(x, g_ref[...], shift_ref[...], scale_ref[...]).astype(BF16)
    y = jnp.zeros_like(x)
    for c in range(D_FF // ff_chunk):
        sl = slice(c * ff_chunk, (c + 1) * ff_chunk)
        a = jnp.maximum(jnp.dot(h, w1_ref[:, sl], preferred_element_type=F32), 0.0)
        y = y + jnp.dot((a * a).astype(BF16), w2_ref[sl, :], preferred_element_type=F32)
    x = x + gate_ref[...] * y
    if with_final_norm:
        x = x * lax.rsqrt(jnp.mean(x * x, axis=-1, keepdims=True) + EPS) * fg_ref[...]
    out_ref[...] = x


def _mlp(x2d, tm, mod, mod_tiles_per_row, norm_g, w1, w2, layer, *, attn_o=None, wo=None,
         final_g=None):
    t = x2d.shape[0]
    const = lambda i: (0, 0)
    this_layer = lambda i: (layer, 0, 0)
    tile = pl.BlockSpec((tm, D_MODEL), lambda i: (i, 0))
    args, specs = [x2d], [tile]
    if attn_o is not None:
        args += [attn_o, wo, mod]
        specs += [pl.BlockSpec((tm, Q_DIM), lambda i: (i, 0)), _resident((Q_DIM, D_MODEL), const),
                  _mod_spec(2, mod_tiles_per_row)]
    args += [norm_g, mod, mod, mod, w1, w2]
    specs += [pl.BlockSpec((1, D_MODEL), const), _mod_spec(3, mod_tiles_per_row),
              _mod_spec(4, mod_tiles_per_row), _mod_spec(5, mod_tiles_per_row),
              _resident((None, D_MODEL, D_FF), this_layer),
              _resident((None, D_FF, D_MODEL), this_layer)]
    if final_g is not None:
        args.append(final_g)
        specs.append(pl.BlockSpec((1, D_MODEL), const))
    return pl.pallas_call(
        functools.partial(_mlp_kernel, with_proj=attn_o is not None,
                          with_final_norm=final_g is not None, ff_chunk=1024),
        grid=(t // tm,),
        in_specs=specs,
        out_specs=tile,
        out_shape=jax.ShapeDtypeStruct(x2d.shape, F32),
        compiler_params=_params(1),
        name="mlp",
    )(*args)


def _gelu(z):
    return 0.5 * z * (1.0 + lax.erf(z * float(np.sqrt(0.5))))


SGU_COL_BLOCK = 2 * SGU_GROUP_DIM


def _sgu_kernel(x_ref, g_ref, shift_ref, scale_ref, gate_ref, win_ref, bin_ref, vg_ref, ws_ref,
                bs_ref, wout_ref, out_ref, v_scr, t_scr, zv_scr):
    tm = x_ref.shape[0]
    x = x_ref[...]
    h = _rms_mod(x, g_ref[...], shift_ref[...], scale_ref[...]).astype(BF16)
    sum_sq = jnp.zeros((tm, 1), F32)
    for cb in range(SGU_DIM // SGU_COL_BLOCK):
        cols = slice(cb * SGU_COL_BLOCK, (cb + 1) * SGU_COL_BLOCK)
        wcols = slice(SGU_DIM + cb * SGU_COL_BLOCK, SGU_DIM + (cb + 1) * SGU_COL_BLOCK)
        zb = _gelu(jnp.dot(h, win_ref[:, wcols], preferred_element_type=F32) + bin_ref[:, wcols])
        zv_scr[:, cols] = zb
        sum_sq = sum_sq + jnp.sum(zb * zb, axis=-1, keepdims=True)
    r = lax.rsqrt(sum_sq * (1.0 / SGU_DIM) + EPS)
    v_scr[...] = (zv_scr[...] * r * vg_ref[...]).astype(BF16)
    for cb in range(SGU_DIM // SGU_COL_BLOCK):
        cols = slice(cb * SGU_COL_BLOCK, (cb + 1) * SGU_COL_BLOCK)
        u = _gelu(jnp.dot(h, win_ref[:, cols], preferred_element_type=F32) + bin_ref[:, cols])
        for gi in range(SGU_COL_BLOCK // SGU_GROUP_DIM):
            grp = cb * (SGU_COL_BLOCK // SGU_GROUP_DIM) + gi
            w_s = ws_ref[grp]
            b_s = bs_ref[grp]
            for c in range(tm // CHUNK):
                rows = slice(c * CHUNK, (c + 1) * CHUNK)
                gcols = slice(grp * SGU_GROUP_DIM, (grp + 1) * SGU_GROUP_DIM)
                sv = jnp.dot(w_s, v_scr[rows, gcols], preferred_element_type=F32) + b_s
                lcols = slice(gi * SGU_GROUP_DIM, (gi + 1) * SGU_GROUP_DIM)
                t_scr[rows, gcols] = (u[rows, lcols] * sv).astype(BF16)
    y = jnp.dot(t_scr[...], wout_ref[...], preferred_element_type=F32)
    out_ref[...] = x + gate_ref[...] * y


def _sgu(x2d, tm, mod, mod_tiles_per_row, norm_g, w_in, b_in, v_g, w_s, b_s, w_out):
    t = x2d.shape[0]
    const = lambda i: (0, 0)
    const3 = lambda i: (0, 0, 0)
    tile = pl.BlockSpec((tm, D_MODEL), lambda i: (i, 0))
    return pl.pallas_call(
        _sgu_kernel,
        grid=(t // tm,),
        in_specs=[
            tile,
            pl.BlockSpec((1, D_MODEL), const),
            _mod_spec(0, mod_tiles_per_row),
            _mod_spec(1, mod_tiles_per_row),
            _mod_spec(2, mod_tiles_per_row),
            _resident((D_MODEL, 2 * SGU_DIM), const),
            pl.BlockSpec((1, 2 * SGU_DIM), const),
            pl.BlockSpec((1, SGU_DIM), const),
            pl.BlockSpec((SGU_GROUPS, CHUNK, CHUNK), const3),
            pl.BlockSpec((SGU_GROUPS, CHUNK, 1), const3),
            _resident((SGU_DIM, D_MODEL), const),
        ],
        out_specs=tile,
        out_shape=jax.ShapeDtypeStruct(x2d.shape, F32),
        scratch_shapes=[pltpu.VMEM((tm, SGU_DIM), BF16), pltpu.VMEM((tm, SGU_DIM), BF16),
                        pltpu.VMEM((tm, SGU_DIM), F32)],
        compiler_params=_params(1),
        name="sgu",
    )(x2d, norm_g, mod, mod, mod, w_in, b_in, v_g, w_s, b_s, w_out)


def _pairs_apart(w, n_heads):
    split = w.reshape(n_heads, HEAD_DIM // 2, 2, *w.shape[1:])
    return jnp.swapaxes(split, 1, 2).reshape(w.shape)


def kernel(x, c, ctx, c_ctx, ada_w, ada_b, mix_norm_g, mlp_norm_g, mlp_w1, mlp_w2, attn_wqkv,
           attn_q_g, attn_k_g, attn_wo, sgu_w_in, sgu_b_in, sgu_v_g, sgu_w_s, sgu_b_s, sgu_w_out,
           final_g):
    b, n, d = x.shape
    lc = ctx.shape[1]
    tm_x, tm_mlp, tm_sgu, tq, tk = QKV_TILE, MLP_TILE, SGU_TILE, FLASH_Q_TILE, FLASH_K_TILE
    tm_c = lc

    cond = jnp.concatenate([c, c_ctx[None, :], jnp.zeros((MOD_ROWS - b - 1, d), F32)], axis=0)
    mods = _adaln(cond, ada_w, ada_b)
    cos_t, sin_t = _rope_tables(n)
    cos_c = jnp.ones((HEAD_DIM // 2, lc), F32)
    sin_c = jnp.zeros((HEAD_DIM // 2, lc), F32)

    x2 = x.reshape(b * n, d)
    c2 = ctx.reshape(b * lc, d)
    row = lambda v: v.reshape(1, -1)
    stacked = dict(w1=mlp_w1, w2=mlp_w2, wo=attn_wo, w_in=sgu_w_in, w_out=sgu_w_out)
    bf16 = None
    for i in range(DEPTH):
        last = i == DEPTH - 1
        use_attn = (i % N_MIXERS) == 0
        j = i // N_MIXERS
        mod_x = mods[i, :b].reshape(b, 1, N_MOD * d)
        mod_c = mods[i, b:b + 1].reshape(1, 1, N_MOD * d)
        x_tpr, c_tpr = n // tm_x, (b * lc) // tm_c
        mix_g = row(mix_norm_g[i])
        mlp_g = row(mlp_norm_g[i])
        fg = row(final_g) if last else None
        if use_attn:
            w_t = attn_wqkv[j].T
            wqkv_t = jnp.concatenate(
                [_pairs_apart(w_t[:Q_DIM + KV_DIM], N_HEADS + N_KV_HEADS), w_t[Q_DIM + KV_DIM:]],
                axis=0).astype(BF16)
            q_g = _pairs_apart(attn_q_g[j], 1)[:, None]
            k_g = _pairs_apart(attn_k_g[j], 1)[:, None]
            q_t, k, vt = _qkv(x2, n, tm_x, mod_x, x_tpr, mix_g, wqkv_t, q_g, k_g, cos_t, sin_t)
            qc_t, kc, vtc = _qkv(c2, lc, tm_c, mod_c, c_tpr, mix_g, wqkv_t, q_g, k_g,
                                 cos_c, sin_c)
            to_cast = () if bf16 else tuple(w.reshape(-1, w.shape[-1]) for w in stacked.values())
            o, casted = _flash(q_t, [(k, vt), (kc, vtc)], n, tq, tk, cast=to_cast)
            if not bf16:
                bf16 = {name: wb.reshape(w.shape)
                        for (name, w), wb in zip(stacked.items(), casted)}
            w1, w2, wo = bf16["w1"], bf16["w2"], bf16["wo"][j]
            x2 = _mlp(x2, tm_mlp, mod_x, n // tm_mlp, mlp_g, w1, w2, i, attn_o=o, wo=wo,
                      final_g=fg)
            if not last:
                oc, _ = _flash(qc_t, [(kc, vtc)], lc, lc, tk)
                c2 = _mlp(c2, tm_c, mod_c, c_tpr, mlp_g, w1, w2, i, attn_o=oc, wo=wo)
        else:
            w1, w2 = bf16["w1"], bf16["w2"]
            sgu_args = (bf16["w_in"][j], row(sgu_b_in[j]), row(sgu_v_g[j]),
                        sgu_w_s[j].astype(BF16), sgu_b_s[j][:, :, None], bf16["w_out"][j])
            x2 = _sgu(x2, tm_sgu, mod_x, n // tm_sgu, mix_g, *sgu_args)
            x2 = _mlp(x2, tm_mlp, mod_x, n // tm_mlp, mlp_g, w1, w2, i, final_g=fg)
            if not last:
                c2 = _sgu(c2, tm_c, mod_c, c_tpr, mix_g, *sgu_args)
                c2 = _mlp(c2, tm_c, mod_c, c_tpr, mlp_g, w1, w2, i)
    return x2.reshape(b, n, d)
```

```python
import functools

import jax
import jax.numpy as jnp
import numpy as np
from jax import lax
from jax.experimental import pallas as pl
from jax.experimental.pallas import tpu as pltpu

D_MODEL = 1024
DEPTH = 2
GRID_W = 64
N_MIXERS = 2
N_HEADS = 8
N_KV_HEADS = 2
HEAD_DIM = 128
GROUP = N_HEADS // N_KV_HEADS
Q_DIM = N_HEADS * HEAD_DIM
KV_DIM = N_KV_HEADS * HEAD_DIM
QKV_DIM = Q_DIM + 2 * KV_DIM
ROPE_AXIS_DIM = HEAD_DIM // 2
ROPE_THETA = 10000.0
CHUNK = 128
SGU_DIM = 3 * D_MODEL
SGU_GROUPS = 8
SGU_GROUP_DIM = SGU_DIM // SGU_GROUPS
D_FF = 4 * D_MODEL
N_MOD = 6
EPS = 1e-6

F32 = jnp.float32
BF16 = jnp.bfloat16

VMEM_LIMIT_BYTES = 56 * 1024 * 1024
MOD_ROWS = 8

QKV_TILE = 1024
MLP_TILE = 1024
SGU_TILE = 4 * CHUNK
FLASH_Q_TILE = 2048
FLASH_K_TILE = 1024

SOFTMAX_Q_SCALE = float(HEAD_DIM ** -0.5 * np.log2(np.e))
NEG_BIG = -1e30
BF16_SUBLANE_TILE = 16


def _params(n_axes):
    return pltpu.CompilerParams(dimension_semantics=("parallel",) * n_axes,
                                vmem_limit_bytes=VMEM_LIMIT_BYTES)


def _resident(shape, index_map):
    return pl.BlockSpec(shape, index_map, pipeline_mode=pl.Buffered(1))


def _rms_mod(x, g, shift, scale):
    gain = g * (1.0 + scale)
    return x * lax.rsqrt(jnp.mean(x * x, axis=-1, keepdims=True) + EPS) * gain + shift


def _mod_spec(k, tiles_per_row):
    return pl.BlockSpec((None, 1, D_MODEL), lambda i: (i // tiles_per_row, 0, k))


def _adaln_kernel(c_ref, w_ref, b_ref, o_ref):
    c = c_ref[...]
    s = c * jax.nn.sigmoid(c)
    o_ref[...] = jnp.dot(s.astype(BF16), w_ref[...].astype(BF16),
                         preferred_element_type=F32) + b_ref[...]


def _adaln(cond, ada_w, ada_b):
    depth = ada_w.shape[0]
    return pl.pallas_call(
        _adaln_kernel,
        grid=(depth, N_MOD),
        in_specs=[
            pl.BlockSpec((MOD_ROWS, D_MODEL), lambda i, j: (0, 0)),
            pl.BlockSpec((None, D_MODEL, D_MODEL), lambda i, j: (i, 0, j)),
            pl.BlockSpec((None, 1, D_MODEL), lambda i, j: (i, 0, j)),
        ],
        out_specs=pl.BlockSpec((None, MOD_ROWS, D_MODEL), lambda i, j: (i, 0, j)),
        out_shape=jax.ShapeDtypeStruct((depth, MOD_ROWS, N_MOD * D_MODEL), F32),
        compiler_params=_params(2),
        name="adaln",
    )(cond, ada_w, ada_b.reshape(depth, 1, N_MOD * D_MODEL))


def _cos_sin_kernel(ang_ref, cos_ref, sin_ref):
    a = ang_ref[...]
    cos_ref[...] = jnp.cos(a)
    sin_ref[...] = jnp.sin(a)


def _rope_tables(n):
    rows_count = n // GRID_W
    freqs = 1.0 / (ROPE_THETA ** (jnp.arange(0, ROPE_AXIS_DIM, 2, dtype=F32) / ROPE_AXIS_DIM))
    pos = jnp.arange(rows_count + GRID_W, dtype=jnp.int32)
    pos = jnp.where(pos < rows_count, pos, pos - rows_count).astype(F32)
    ang = pos[:, None] * freqs[None, :]
    cos_s, sin_s = pl.pallas_call(
        _cos_sin_kernel,
        out_shape=(jax.ShapeDtypeStruct(ang.shape, F32),) * 2,
        name="rope_cos_sin",
    )(ang)

    def expand(t):
        by_row = jnp.repeat(t[:rows_count], GRID_W, axis=0)
        by_col = jnp.tile(t[rows_count:], (rows_count, 1))
        return jnp.concatenate([by_row, by_col], axis=-1)

    return expand(cos_s).T, expand(sin_s).T


Q_HEADS_PER_DOT = 4


def _qkv_kernel(x_ref, g_ref, shift_ref, scale_ref, wt_ref, qg_ref, kg_ref, cos_ref, sin_ref,
                qt_ref, k_ref, vt_ref):
    h =_rms_mod(x_ref[...], g_ref[...], shift_ref[...], scale_ref[...]).astype(BF16)
    cos = cos_ref[...]
    sin = sin_ref[...]
    half = HEAD_DIM // 2
    nt_dims = (((1,), (1,)), ((), ()))

    def norm_rope(t, g):
        t = t * lax.rsqrt(jnp.mean(t * t, axis=0, keepdims=True) + EPS) * g
        x1, x2 = t[:half], t[half:]
        return x1 * cos - x2 * sin, x1 * sin + x2 * cos

    qg = qg_ref[...]
    for blk in range(N_HEADS // Q_HEADS_PER_DOT):
        rows = slice(blk * Q_HEADS_PER_DOT * HEAD_DIM, (blk + 1) * Q_HEADS_PER_DOT * HEAD_DIM)
        yt = lax.dot_general(wt_ref[rows, :], h, nt_dims, preferred_element_type=F32)
        for j in range(Q_HEADS_PER_DOT):
            r0 = (blk * Q_HEADS_PER_DOT + j) * HEAD_DIM
            o1, o2 = norm_rope(yt[j * HEAD_DIM:(j + 1) * HEAD_DIM], qg)
            qt_ref[r0:r0 + half, :] = (o1 * SOFTMAX_Q_SCALE).astype(BF16)
            qt_ref[r0 + half:r0 + HEAD_DIM, :] = (o2 * SOFTMAX_Q_SCALE).astype(BF16)
    yt = lax.dot_general(wt_ref[Q_DIM:, :], h, nt_dims, preferred_element_type=F32)
    kg = kg_ref[...]
    for hd in range(N_KV_HEADS):
        o1, o2 = norm_rope(yt[hd * HEAD_DIM:(hd + 1) * HEAD_DIM], kg)
        k_ref[hd] = jnp.concatenate([o1, o2], axis=0).T.astype(BF16)
        v_t = yt[KV_DIM + hd * HEAD_DIM:KV_DIM + (hd + 1) * HEAD_DIM]
        vt_ref[hd] = v_t.astype(BF16)


def _qkv(x2d, seq, tm, mod, mod_tiles_per_row, norm_g, wqkv_t, q_g, k_g, cos_t, sin_t):
    t = x2d.shape[0]
    tpb = seq // tm
    const = lambda i: (0, 0)
    return pl.pallas_call(
        _qkv_kernel,
        grid=(t // tm,),
        in_specs=[
            pl.BlockSpec((tm, D_MODEL), lambda i: (i, 0)),
            pl.BlockSpec((1, D_MODEL), const),
            _mod_spec(0, mod_tiles_per_row),
            _mod_spec(1, mod_tiles_per_row),
            _resident((QKV_DIM, D_MODEL), const),
            pl.BlockSpec((HEAD_DIM, 1), const),
            pl.BlockSpec((HEAD_DIM, 1), const),
            pl.BlockSpec((HEAD_DIM // 2, tm), lambda i: (0, i % tpb)),
            pl.BlockSpec((HEAD_DIM // 2, tm), lambda i: (0, i % tpb)),
        ],
        out_specs=[
            pl.BlockSpec((Q_DIM, tm), lambda i: (0, i)),
            pl.BlockSpec((None, N_KV_HEADS, tm, HEAD_DIM), lambda i: (i // tpb, 0, i % tpb, 0)),
            pl.BlockSpec((None, N_KV_HEADS, HEAD_DIM, tm), lambda i: (i // tpb, 0, 0, i % tpb)),
        ],
        out_shape=[
            jax.ShapeDtypeStruct((Q_DIM, t), BF16),
            jax.ShapeDtypeStruct((t // seq, N_KV_HEADS, seq, HEAD_DIM), BF16),
            jax.ShapeDtypeStruct((t // seq, N_KV_HEADS, HEAD_DIM, seq), BF16),
        ],
        compiler_params=_params(1),
        name="qkv_proj",
    )(x2d, norm_g, mod, mod, wqkv_t, q_g, k_g, cos_t, sin_t)


MAX_TILES_PER_HEAD = 16
L_MIN = 2.0 ** -60


def _flash_kernel(*refs, tiles, n_cast):
    n_src = len(tiles)
    n_in = 1 + 2 * n_src + n_cast
    q_ref = refs[0]
    k_refs = refs[1:1 + 2 * n_src:2]
    vt_refs = refs[2:2 + 2 * n_src:2]
    o_ref = refs[n_in]
    shift_scr, acc_scr, l_scr, kmax_scr = refs[n_in + 1 + n_cast:]

    for w_ref, w_bf16_ref in zip(refs[n_in - n_cast:n_in], refs[n_in + 1:n_in + 1 + n_cast]):
        w_bf16_ref[...] = w_ref[...].astype(BF16)

    @pl.when(pl.program_id(2) == 0)
    def _():
        kmax2 = jnp.zeros((1, 1), F32)
        for k_ref, (tk, nk) in zip(k_refs, tiles):
            def body(t, n2, k_ref=k_ref, tk=tk):
                kk = k_ref[pl.ds(pl.multiple_of(t * tk, tk), tk), :].astype(F32)
                return jnp.maximum(n2, jnp.sum(kk * kk, axis=1, keepdims=True))

            n2 = lax.fori_loop(0, nk, body, jnp.zeros((tk, 1), F32))
            kmax2 = jnp.maximum(kmax2, jnp.max(n2, axis=0, keepdims=True))
        kmax_scr[...] = jnp.broadcast_to(jnp.sqrt(kmax2), kmax_scr.shape)

    def operands(src, g, t):
        tk = tiles[src][0]
        ks = t * tk if isinstance(t, int) else pl.multiple_of(t * tk, tk)
        qs = g * HEAD_DIM if isinstance(g, int) else pl.multiple_of(g * HEAD_DIM, HEAD_DIM)
        q_t = q_ref[pl.ds(qs, HEAD_DIM), :]
        return k_refs[src][pl.ds(ks, tk), :], q_t, vt_refs[src][:, pl.ds(ks, tk)]

    def finalize():
        l_min = None
        for g in range(GROUP):
            l = l_scr[g]
            o = acc_scr[g] * (1.0 / l)
            o_ref[:, g * HEAD_DIM:(g + 1) * HEAD_DIM] = o.astype(BF16).T
            l_min = l if l_min is None else jnp.minimum(l_min, l)
        return jnp.min(l_min)

    for g in range(GROUP):
        qf = q_ref[g * HEAD_DIM:(g + 1) * HEAD_DIM, :].astype(F32)
        shift_scr[g] = jnp.sqrt(jnp.sum(qf * qf, axis=0, keepdims=True)) * kmax_scr[...]

    def bounded_step(src, g, t):
        k_tile, q_t, vt_tile = operands(src, g, t)
        s = jnp.dot(k_tile, q_t, preferred_element_type=F32)
        p = jnp.exp2(s - shift_scr[g])
        l_tile = jnp.sum(p, axis=0, keepdims=True)
        pv = jnp.dot(vt_tile, p.astype(BF16), preferred_element_type=F32)
        if (src, t) == (0, 0):
            acc_scr[g], l_scr[g] = pv, l_tile
        else:
            acc_scr[g] += pv
            l_scr[g] += l_tile

    def per_head(step_fn):
        def body(g, carry):
            for src, (_, nk) in enumerate(tiles):
                for t in range(nk):
                    step_fn(src, g, t)
            return carry

        lax.fori_loop(0, GROUP, body, 0)

    per_head(bounded_step)
    l_min = finalize()

    @pl.when(jnp.logical_not(l_min >= L_MIN))
    def _():
        shift_scr[...] = jnp.full(shift_scr.shape, NEG_BIG, F32)
        acc_scr[...] = jnp.zeros(acc_scr.shape, F32)
        l_scr[...] = jnp.zeros(l_scr.shape, F32)

        def online_step(src, g, t):
            k_tile, q_t, vt_tile = operands(src, g, t)
            s = jnp.dot(k_tile, q_t, preferred_element_type=F32)
            m_old = shift_scr[g]
            m_new = jnp.maximum(m_old, jnp.max(s, axis=0, keepdims=True))
            shift_scr[g] = m_new
            alpha = jnp.exp2(m_old - m_new)
            p = jnp.exp2(s - m_new)
            l_scr[g] = alpha * l_scr[g] + jnp.sum(p, axis=0, keepdims=True)
            pv = jnp.dot(vt_tile, p.astype(BF16), preferred_element_type=F32)
            acc_scr[g] = alpha * acc_scr[g] + pv

        for src, (_, nk) in enumerate(tiles):
            def body(u, carry, src=src, nk=nk):
                g = u // nk
                online_step(src, g, u - g * nk)
                return carry

            lax.fori_loop(0, GROUP * nk, body, 0)
        finalize()


def _flash(q_t, sources, seq_q, tq, tk, cast=()):
    b = sources[0][0].shape[0]
    nq = seq_q // tq
    gw = GROUP * HEAD_DIM
    n_grid = b * N_KV_HEADS * nq
    grid_pos = lambda bi, h, i: ((bi * N_KV_HEADS + h) * nq + i, 0)
    cast_specs = []
    for w in cast:
        rows, rem = divmod(w.shape[0], n_grid)
        assert rem == 0 and rows % BF16_SUBLANE_TILE == 0, w.shape
        cast_specs.append(pl.BlockSpec((rows, w.shape[1]), grid_pos))
    kv_args, kv_specs, tiles = [], [], []
    for k, vt in sources:
        lk = k.shape[2]
        rows = min(tk, lk)
        tiles.append((rows, lk // rows))
        kv_args += [k, vt]
        assert sum(nk for _, nk in tiles) <= MAX_TILES_PER_HEAD, tiles
        kv_specs += [pl.BlockSpec((None, None, lk, HEAD_DIM), lambda bi, h, i: (bi, h, 0, 0)),
                     pl.BlockSpec((None, None, HEAD_DIM, lk), lambda bi, h, i: (bi, h, 0, 0))]
    out = pl.pallas_call(
        functools.partial(_flash_kernel, tiles=tuple(tiles), n_cast=len(cast)),
        grid=(b, N_KV_HEADS, nq),
        in_specs=([pl.BlockSpec((gw, tq), lambda bi, h, i: (h, bi * nq + i))] + kv_specs
                  + cast_specs),
        out_specs=[pl.BlockSpec((tq, gw), lambda bi, h, i: (bi * nq + i, h))] + cast_specs,
        out_shape=([jax.ShapeDtypeStruct((q_t.shape[1], q_t.shape[0]), BF16)]
                   + [jax.ShapeDtypeStruct(w.shape, BF16) for w in cast]),
        scratch_shapes=[pltpu.VMEM((GROUP, 1, tq), F32), pltpu.VMEM((GROUP, HEAD_DIM, tq), F32),
                        pltpu.VMEM((GROUP, 1, tq), F32), pltpu.VMEM((1, tq), F32)],
        compiler_params=pltpu.CompilerParams(
            dimension_semantics=("parallel", "parallel", "arbitrary"),
            vmem_limit_bytes=VMEM_LIMIT_BYTES),
        name="flash_attn",
    )(q_t, *kv_args, *cast)
    return out[0], out[1:]


def _mlp_kernel(*refs, with_proj, with_final_norm, ff_chunk):
    refs = list(refs)
    x_ref = refs.pop(0)
    if with_proj:
        o_ref, wo_ref, gate_mix_ref = refs.pop(0), refs.pop(0), refs.pop(0)
    g_ref, shift_ref, scale_ref, gate_ref, w1_ref, w2_ref = refs[:6]
    refs = refs[6:]
    if with_final_norm:
        fg_ref = refs.pop(0)
    (out_ref,) = refs

    x = x_ref[...]
    if with_proj:
        x = x + gate_mix_ref[...] * jnp.dot(o_ref[...], wo_ref[...], preferred_element_type=F32)
    h = _rms_mod(x, g_ref[...], shift_ref[...], scale_ref[...]).astype(BF16)
    y = jnp.zeros_like(x)
    for c in range(D_FF // ff_chunk):
        sl = slice(c * ff_chunk, (c + 1) * ff_chunk)
        a = jnp.maximum(jnp.dot(h, w1_ref[:, sl], preferred_element_type=F32), 0.0)
        y = y + jnp.dot((a * a).astype(BF16), w2_ref[sl, :], preferred_element_type=F32)
    x = x + gate_ref[...] * y
    if with_final_norm:
        x = x * lax.rsqrt(jnp.mean(x * x, axis=-1, keepdims=True) + EPS) * fg_ref[...]
    out_ref[...] = x


def _mlp(x2d, tm, mod, mod_tiles_per_row, norm_g, w1, w2, layer, *, attn_o=None, wo=None,
         final_g=None):
    t = x2d.shape[0]
    const = lambda i: (0, 0)
    this_layer = lambda i: (layer, 0, 0)
    tile = pl.BlockSpec((tm, D_MODEL), lambda i: (i, 0))
    args, specs = [x2d], [tile]
    if attn_o is not None:
        args += [attn_o, wo, mod]
        specs += [pl.BlockSpec((tm, Q_DIM), lambda i: (i, 0)), _resident((Q_DIM, D_MODEL), const),
                  _mod_spec(2, mod_tiles_per_row)]
    args += [norm_g, mod, mod, mod, w1, w2]
    specs += [pl.BlockSpec((1, D_MODEL), const), _mod_spec(3, mod_tiles_per_row),
              _mod_spec(4, mod_tiles_per_row), _mod_spec(5, mod_tiles_per_row),
              _resident((None, D_MODEL, D_FF), this_layer),
              _resident((None, D_FF, D_MODEL), this_layer)]
    if final_g is not None:
        args.append(final_g)
        specs.append(pl.BlockSpec((1, D_MODEL), const))
    return pl.pallas_call(
        functools.partial(_mlp_kernel, with_proj=attn_o is not None,
                          with_final_norm=final_g is not None, ff_chunk=1024),
        grid=(t // tm,),
        in_specs=specs,
        out_specs=tile,
        out_shape=jax.ShapeDtypeStruct(x2d.shape, F32),
        compiler_params=_params(1),
        name="mlp",
    )(*args)


def _gelu(z):
    return 0.5 * z * (1.0 + lax.erf(z * float(np.sqrt(0.5))))


SGU_COL_BLOCK = 2 * SGU_GROUP_DIM


def _sgu_kernel(x_ref, g_ref, shift_ref, scale_ref, gate_ref, win_ref, bin_ref, vg_ref, ws_ref,
                bs_ref, wout_ref, out_ref, v_scr, t_scr, zv_scr):
    tm = x_ref.shape[0]
    x = x_ref[...]
    h = _rms_mod(x, g_ref[...], shift_ref[...], scale_ref[...]).astype(BF16)
    sum_sq = jnp.zeros((tm, 1), F32)
    for cb in range(SGU_DIM // SGU_COL_BLOCK):
        cols = slice(cb * SGU_COL_BLOCK, (cb + 1) * SGU_COL_BLOCK)
        wcols = slice(SGU_DIM + cb * SGU_COL_BLOCK, SGU_DIM + (cb + 1) * SGU_COL_BLOCK)
        zb = _gelu(jnp.dot(h, win_ref[:, wcols], preferred_element_type=F32) + bin_ref[:, wcols])
        zv_scr[:, cols] = zb
        sum_sq = sum_sq + jnp.sum(zb * zb, axis=-1, keepdims=True)
    r = lax.rsqrt(sum_sq * (1.0 / SGU_DIM) + EPS)
    v_scr[...] = (zv_scr[...] * r * vg_ref[...]).astype(BF16)
    for cb in range(SGU_DIM // SGU_COL_BLOCK):
        cols = slice(cb * SGU_COL_BLOCK, (cb + 1) * SGU_COL_BLOCK)
        u = _gelu(jnp.dot(h, win_ref[:, cols], preferred_element_type=F32) + bin_ref[:, cols])
        for gi in range(SGU_COL_BLOCK // SGU_GROUP_DIM):
            grp = cb * (SGU_COL_BLOCK // SGU_GROUP_DIM) + gi
            w_s = ws_ref[grp]
            b_s = bs_ref[grp]
            for c in range(tm // CHUNK):
                rows = slice(c * CHUNK, (c + 1) * CHUNK)
                gcols = slice(grp * SGU_GROUP_DIM, (grp + 1) * SGU_GROUP_DIM)
                sv = jnp.dot(w_s, v_scr[rows, gcols], preferred_element_type=F32) + b_s
                lcols = slice(gi * SGU_GROUP_DIM, (gi + 1) * SGU_GROUP_DIM)
                t_scr[rows, gcols] = (u[rows, lcols] * sv).astype(BF16)
    y = jnp.dot(t_scr[...], wout_ref[...], preferred_element_type=F32)
    out_ref[...] = x + gate_ref[...] * y


def _sgu(x2d, tm, mod, mod_tiles_per_row, norm_g, w_in, b_in, v_g, w_s, b_s, w_out):
    t = x2d.shape[0]
    const = lambda i: (0, 0)
    const3 = lambda i: (0, 0, 0)
    tile = pl.BlockSpec((tm, D_MODEL), lambda i: (i, 0))
    return pl.pallas_call(
        _sgu_kernel,
        grid=(t // tm,),
        in_specs=[
            tile,
            pl.BlockSpec((1, D_MODEL), const),
            _mod_spec(0, mod_tiles_per_row),
            _mod_spec(1, mod_tiles_per_row),
            _mod_spec(2, mod_tiles_per_row),
            _resident((D_MODEL, 2 * SGU_DIM), const),
            pl.BlockSpec((1, 2 * SGU_DIM), const),
            pl.BlockSpec((1, SGU_DIM), const),
            pl.BlockSpec((SGU_GROUPS, CHUNK, CHUNK), const3),
            pl.BlockSpec((SGU_GROUPS, CHUNK, 1), const3),
            _resident((SGU_DIM, D_MODEL), const),
        ],
        out_specs=tile,
        out_shape=jax.ShapeDtypeStruct(x2d.shape, F32),
        scratch_shapes=[pltpu.VMEM((tm, SGU_DIM), BF16), pltpu.VMEM((tm, SGU_DIM), BF16),
                        pltpu.VMEM((tm, SGU_DIM), F32)],
        compiler_params=_params(1),
        name="sgu",
    )(x2d, norm_g, mod, mod, mod, w_in, b_in, v_g, w_s, b_s, w_out)


def _pairs_apart(w, n_heads):
    split = w.reshape(n_heads, HEAD_DIM // 2, 2, *w.shape[1:])
    return jnp.swapaxes(split, 1, 2).reshape(w.shape)


def kernel(x, c, ctx, c_ctx, ada_w, ada_b, mix_norm_g, mlp_norm_g, mlp_w1, mlp_w2, attn_wqkv,
           attn_q_g, attn_k_g, attn_wo, sgu_w_in, sgu_b_in, sgu_v_g, sgu_w_s, sgu_b_s, sgu_w_out,
           final_g):
    b, n, d = x.shape
    lc = ctx.shape[1]
    tm_x, tm_mlp, tm_sgu, tq, tk = QKV_TILE, MLP_TILE, SGU_TILE, FLASH_Q_TILE, FLASH_K_TILE
    tm_c = lc

    cond = jnp.concatenate([c, c_ctx[None, :], jnp.zeros((MOD_ROWS - b - 1, d), F32)], axis=0)
    mods = _adaln(cond, ada_w, ada_b)
    cos_t, sin_t = _rope_tables(n)
    cos_c = jnp.ones((HEAD_DIM // 2, lc), F32)
    sin_c = jnp.zeros((HEAD_DIM // 2, lc), F32)

    x2 = x.reshape(b * n, d)
    c2 = ctx.reshape(b * lc, d)
    row = lambda v: v.reshape(1, -1)
    stacked = dict(w1=mlp_w1, w2=mlp_w2, wo=attn_wo, w_in=sgu_w_in, w_out=sgu_w_out)
    bf16 = None
    for i in range(DEPTH):
        last = i == DEPTH - 1
        use_attn = (i % N_MIXERS) == 0
        j = i // N_MIXERS
        mod_x = mods[i, :b].reshape(b, 1, N_MOD * d)
        mod_c = mods[i, b:b + 1].reshape(1, 1, N_MOD * d)
        x_tpr, c_tpr = n // tm_x, (b * lc) // tm_c
        mix_g = row(mix_norm_g[i])
        mlp_g = row(mlp_norm_g[i])
        fg = row(final_g) if last else None
        if use_attn:
            w_t = attn_wqkv[j].T
            wqkv_t = jnp.concatenate(
                [_pairs_apart(w_t[:Q_DIM + KV_DIM], N_HEADS + N_KV_HEADS), w_t[Q_DIM + KV_DIM:]],
                axis=0).astype(BF16)
            q_g = _pairs_apart(attn_q_g[j], 1)[:, None]
            k_g = _pairs_apart(attn_k_g[j], 1)[:, None]
            q_t, k, vt = _qkv(x2, n, tm_x, mod_x, x_tpr, mix_g, wqkv_t, q_g, k_g, cos_t, sin_t)
            qc_t, kc, vtc = _qkv(c2, lc, tm_c, mod_c, c_tpr, mix_g, wqkv_t, q_g, k_g,
                                 cos_c, sin_c)
            to_cast = () if bf16 else tuple(w.reshape(-1, w.shape[-1]) for w in stacked.values())
            o, casted = _flash(q_t, [(k, vt), (kc, vtc)], n, tq, tk, cast=to_cast)
            if not bf16:
                bf16 = {name: wb.reshape(w.shape)
                        for (name, w), wb in zip(stacked.items(), casted)}
            w1, w2, wo = bf16["w1"], bf16["w2"], bf16["wo"][j]
            x2 = _mlp(x2, tm_mlp, mod_x, n // tm_mlp, mlp_g, w1, w2, i, attn_o=o, wo=wo,
                      final_g=fg)
            if not last:
                oc, _ = _flash(qc_t, [(kc, vtc)], lc, lc, tk)
                c2 = _mlp(c2, tm_c, mod_c, c_tpr, mlp_g, w1, w2, i, attn_o=oc, wo=wo)
        else:
            w1, w2 = bf16["w1"], bf16["w2"]
            sgu_args = (bf16["w_in"][j], row(sgu_b_in[j]), row(sgu_v_g[j]),
                        sgu_w_s[j].astype(BF16), sgu_b_s[j][:, :, None], bf16["w_out"][j])
            x2 = _sgu(x2, tm_sgu, mod_x, n // tm_sgu, mix_g, *sgu_args)
            x2 = _mlp(x2, tm_mlp, mod_x, n // tm_mlp, mlp_g, w1, w2, i, final_g=fg)
            if not last:
                c2 = _sgu(c2, tm_c, mod_c, c_tpr, mix_g, *sgu_args)
                c2 = _mlp(c2, tm_c, mod_c, c_tpr, mlp_g, w1, w2, i)
    return x2.reshape(b, n, d)
```

```python
import functools

import jax
import jax.numpy as jnp
import numpy as np
from jax import lax
from jax.experimental import pallas as pl
from jax.experimental.pallas import tpu as pltpu

D_MODEL = 1024
DEPTH = 2
GRID_W = 64
N_MIXERS = 2
N_HEADS = 8
N_KV_HEADS = 2
HEAD_DIM = 128
GROUP = N_HEADS // N_KV_HEADS
Q_DIM = N_HEADS * HEAD_DIM
KV_DIM = N_KV_HEADS * HEAD_DIM
QKV_DIM = Q_DIM + 2 * KV_DIM
ROPE_AXIS_DIM = HEAD_DIM // 2
ROPE_THETA = 10000.0
CHUNK = 128
SGU_DIM = 3 * D_MODEL
SGU_GROUPS = 8
SGU_GROUP_DIM = SGU_DIM // SGU_GROUPS
D_FF = 4 * D_MODEL
N_MOD = 6
EPS = 1e-6

F32 = jnp.float32
BF16 = jnp.bfloat16

VMEM_LIMIT_BYTES = 56 * 1024 * 1024
MOD_ROWS = 8

QKV_TILE = 1024
MLP_TILE = 1024
SGU_TILE = 4 * CHUNK
FLASH_Q_TILE = 2048
FLASH_K_TILE = 1024

SOFTMAX_Q_SCALE = float(HEAD_DIM ** -0.5 * np.log2(np.e))
NEG_BIG = -1e30
BF16_SUBLANE_TILE = 16


def _params(n_axes):
    return pltpu.CompilerParams(dimension_semantics=("parallel",) * n_axes,
                                vmem_limit_bytes=VMEM_LIMIT_BYTES)


def _resident(shape, index_map):
    return pl.BlockSpec(shape, index_map, pipeline_mode=pl.Buffered(1))


def _rms_mod(x, g, shift, scale):
    gain = g * (1.0 + scale)
    return x * lax.rsqrt(jnp.mean(x * x, axis=-1, keepdims=True) + EPS) * gain + shift


def _mod_spec(k, tiles_per_row):
    return pl.BlockSpec((None, 1, D_MODEL), lambda i: (i // tiles_per_row, 0, k))


def _adaln_kernel(c_ref, w_ref, b_ref, o_ref):
    c = c_ref[...]
    s = c * jax.nn.sigmoid(c)
    o_ref[...] = jnp.dot(s.astype(BF16), w_ref[...].astype(BF16),
                         preferred_element_type=F32) + b_ref[...]


def _adaln(cond, ada_w, ada_b):
    depth = ada_w.shape[0]
    return pl.pallas_call(
        _adaln_kernel,
        grid=(depth, N_MOD),
        in_specs=[
            pl.BlockSpec((MOD_ROWS, D_MODEL), lambda i, j: (0, 0)),
            pl.BlockSpec((None, D_MODEL, D_MODEL), lambda i, j: (i, 0, j)),
            pl.BlockSpec((None, 1, D_MODEL), lambda i, j: (i, 0, j)),
        ],
        out_specs=pl.BlockSpec((None, MOD_ROWS, D_MODEL), lambda i, j: (i, 0, j)),
        out_shape=jax.ShapeDtypeStruct((depth, MOD_ROWS, N_MOD * D_MODEL), F32),
        compiler_params=_params(2),
        name="adaln",
    )(cond, ada_w, ada_b.reshape(depth, 1, N_MOD * D_MODEL))


def _cos_sin_kernel(ang_ref, cos_ref, sin_ref):
    a = ang_ref[...]
    cos_ref[...] = jnp.cos(a)
    sin_ref[...] = jnp.sin(a)


def _rope_tables(n):
    rows_count = n // GRID_W
    freqs = 1.0 / (ROPE_THETA ** (jnp.arange(0, ROPE_AXIS_DIM, 2, dtype=F32) / ROPE_AXIS_DIM))
    pos = jnp.arange(rows_count + GRID_W, dtype=jnp.int32)
    pos = jnp.where(pos < rows_count, pos, pos - rows_count).astype(F32)
    ang = pos[:, None] * freqs[None, :]
    cos_s, sin_s = pl.pallas_call(
        _cos_sin_kernel,
        out_shape=(jax.ShapeDtypeStruct(ang.shape, F32),) * 2,
        name="rope_cos_sin",
    )(ang)

    def expand(t):
        by_row = jnp.repeat(t[:rows_count], GRID_W, axis=0)
        by_col = jnp.tile(t[rows_count:], (rows_count, 1))
        return jnp.concatenate([by_row, by_col], axis=-1)

    return expand(cos_s).T, expand(sin_s).T


Q_HEADS_PER_DOT = 4


def _qkv_kernel(x_ref, g_ref, shift_ref, scale_ref, wt_ref, qg_ref, kg_ref, cos_ref, sin_ref,
                qt_ref, k_ref, vt_ref):
    h = _rms_mod(x_ref[...], g_ref[...], shift_ref[...], scale_ref[...]).astype(BF16)
    cos = cos_ref[...]
    sin = sin_ref[...]
    half = HEAD_DIM // 2
    nt_dims = (((1,), (1,)), ((), ()))

    def norm_rope(t, g):
        t = t * lax.rsqrt(jnp.mean(t * t, axis=0, keepdims=True) + EPS) * g
        x1, x2 = t[:half], t[half:]
        return x1 * cos - x2 * sin, x1 * sin + x2 * cos

    qg = qg_ref[...] * SOFTMAX_Q_SCALE
    for blk in range(N_HEADS // Q_HEADS_PER_DOT):
        rows = slice(blk * Q_HEADS_PER_DOT * HEAD_DIM, (blk + 1) * Q_HEADS_PER_DOT * HEAD_DIM)
        yt = lax.dot_general(wt_ref[rows, :], h, nt_dims, preferred_element_type=F32)
        for j in range(Q_HEADS_PER_DOT):
            r0 = (blk * Q_HEADS_PER_DOT + j) * HEAD_DIM
            o1, o2 = norm_rope(yt[j * HEAD_DIM:(j + 1) * HEAD_DIM], qg)
            qt_ref[r0:r0 + half, :] = o1.astype(BF16)
            qt_ref[r0 + half:r0 + HEAD_DIM, :] = o2.astype(BF16)
    yt = lax.dot_general(wt_ref[Q_DIM:, :], h, nt_dims, preferred_element_type=F32)
    kg = kg_ref[...]
    for hd in range(N_KV_HEADS):
        o1, o2 = norm_rope(yt[hd * HEAD_DIM:(hd + 1) * HEAD_DIM], kg)
        k_ref[hd] = jnp.concatenate([o1, o2], axis=0).T.astype(BF16)
        v_t = yt[KV_DIM + hd * HEAD_DIM:KV_DIM + (hd + 1) * HEAD_DIM]
        vt_ref[hd] = v_t.astype(BF16)


def _qkv(x2d, seq, tm, mod, mod_tiles_per_row, norm_g, wqkv_t, q_g, k_g, cos_t, sin_t):
    t = x2d.shape[0]
    tpb = seq // tm
    const = lambda i: (0, 0)
    return pl.pallas_call(
        _qkv_kernel,
        grid=(t // tm,),
        in_specs=[
            pl.BlockSpec((tm, D_MODEL), lambda i: (i, 0)),
            pl.BlockSpec((1, D_MODEL), const),
            _mod_spec(0, mod_tiles_per_row),
            _mod_spec(1, mod_tiles_per_row),
            _resident((QKV_DIM, D_MODEL), const),
            pl.BlockSpec((HEAD_DIM, 1), const),
            pl.BlockSpec((HEAD_DIM, 1), const),
            pl.BlockSpec((HEAD_DIM // 2, tm), lambda i: (0, i % tpb)),
            pl.BlockSpec((HEAD_DIM // 2, tm), lambda i: (0, i % tpb)),
        ],
        out_specs=[
            pl.BlockSpec((Q_DIM, tm), lambda i: (0, i)),
            pl.BlockSpec((None, N_KV_HEADS, tm, HEAD_DIM), lambda i: (i // tpb, 0, i % tpb, 0)),
            pl.BlockSpec((None, N_KV_HEADS, HEAD_DIM, tm), lambda i: (i // tpb, 0, 0, i % tpb)),
        ],
        out_shape=[
            jax.ShapeDtypeStruct((Q_DIM, t), BF16),
            jax.ShapeDtypeStruct((t // seq, N_KV_HEADS, seq, HEAD_DIM), BF16),
            jax.ShapeDtypeStruct((t // seq, N_KV_HEADS, HEAD_DIM, seq), BF16),
        ],
        compiler_params=_params(1),
        name="qkv_proj",
    )(x2d, norm_g, mod, mod, wqkv_t, q_g, k_g, cos_t, sin_t)


MAX_TILES_PER_HEAD = 16
L_MIN = 2.0 ** -60


def _flash_kernel(*refs, tiles, n_cast):
    n_src = len(tiles)
    n_in = 1 + 2 * n_src + n_cast
    q_ref = refs[0]
    k_refs = refs[1:1 + 2 * n_src:2]
    vt_refs = refs[2:2 + 2 * n_src:2]
    o_ref = refs[n_in]
    shift_scr, acc_scr, l_scr, kmax_scr = refs[n_in + 1 + n_cast:]

    for w_ref, w_bf16_ref in zip(refs[n_in - n_cast:n_in], refs[n_in + 1:n_in + 1 + n_cast]):
        w_bf16_ref[...] = w_ref[...].astype(BF16)

    @pl.when(pl.program_id(2) == 0)
    def _():
        kmax2 = jnp.zeros((1, 1), F32)
        for k_ref, (tk, nk) in zip(k_refs, tiles):
            def body(t, n2, k_ref=k_ref, tk=tk):
                kk = k_ref[pl.ds(pl.multiple_of(t * tk, tk), tk), :].astype(F32)
                return jnp.maximum(n2, jnp.sum(kk * kk, axis=1, keepdims=True))

            n2 = lax.fori_loop(0, nk, body, jnp.zeros((tk, 1), F32))
            kmax2 = jnp.maximum(kmax2, jnp.max(n2, axis=0, keepdims=True))
        kmax_scr[...] = jnp.broadcast_to(jnp.sqrt(kmax2), kmax_scr.shape)

    def operands(src, g, t):
        tk = tiles[src][0]
        ks = t * tk if isinstance(t, int) else pl.multiple_of(t * tk, tk)
        qs = g * HEAD_DIM if isinstance(g, int) else pl.multiple_of(g * HEAD_DIM, HEAD_DIM)
        q_t = q_ref[pl.ds(qs, HEAD_DIM), :]
        return k_refs[src][pl.ds(ks, tk), :], q_t, vt_refs[src][:, pl.ds(ks, tk)]

    def finalize():
        l_min = None
        for g in range(GROUP):
            l = l_scr[g]
            o = acc_scr[g] * (1.0 / l)
            o_ref[:, g * HEAD_DIM:(g + 1) * HEAD_DIM] = o.astype(BF16).T
            l_min = l if l_min is None else jnp.minimum(l_min, l)
        return jnp.min(l_min)

    for g in range(GROUP):
        qf = q_ref[g * HEAD_DIM:(g + 1) * HEAD_DIM, :].astype(F32)
        shift_scr[g] = jnp.sqrt(jnp.sum(qf * qf, axis=0, keepdims=True)) * kmax_scr[...]

    def bounded_step(src, g, t):
        k_tile, q_t, vt_tile = operands(src, g, t)
        s = jnp.dot(k_tile, q_t, preferred_element_type=F32)
        p = jnp.exp2(s - shift_scr[g])
        l_tile = jnp.sum(p, axis=0, keepdims=True)
        pv = jnp.dot(vt_tile, p.astype(BF16), preferred_element_type=F32)
        if (src, t) == (0, 0):
            acc_scr[g], l_scr[g] = pv, l_tile
        else:
            acc_scr[g] += pv
            l_scr[g] += l_tile

    def per_head(step_fn):
        def body(g, carry):
            for src, (_, nk) in enumerate(tiles):
                for t in range(nk):
                    step_fn(src, g, t)
            return carry

        lax.fori_loop(0, GROUP, body, 0)

    per_head(bounded_step)
    l_min = finalize()

    @pl.when(jnp.logical_not(l_min >= L_MIN))
    def _():
        shift_scr[...] = jnp.full(shift_scr.shape, NEG_BIG, F32)
        acc_scr[...] = jnp.zeros(acc_scr.shape, F32)
        l_scr[...] = jnp.zeros(l_scr.shape, F32)

        def online_step(src, g, t):
            k_tile, q_t, vt_tile = operands(src, g, t)
            s = jnp.dot(k_tile, q_t, preferred_element_type=F32)
            m_old = shift_scr[g]
            m_new = jnp.maximum(m_old, jnp.max(s, axis=0, keepdims=True))
            shift_scr[g] = m_new
            alpha = jnp.exp2(m_old - m_new)
            p = jnp.exp2(s - m_new)
            l_scr[g] = alpha * l_scr[g] + jnp.sum(p, axis=0, keepdims=True)
            pv = jnp.dot(vt_tile, p.astype(BF16), preferred_element_type=F32)
            acc_scr[g] = alpha * acc_scr[g] + pv

        for src, (_, nk) in enumerate(tiles):
            def body(u, carry, src=src, nk=nk):
                g = u // nk
                online_step(src, g, u - g * nk)
                return carry

            lax.fori_loop(0, GROUP * nk, body, 0)
        finalize()


def _flash(q_t, sources, seq_q, tq, tk, cast=()):
    b = sources[0][0].shape[0]
    nq = seq_q // tq
    gw = GROUP * HEAD_DIM
    n_grid = b * N_KV_HEADS * nq
    grid_pos = lambda bi, h, i: ((bi * N_KV_HEADS + h) * nq + i, 0)
    cast_specs = []
    for w in cast:
        rows, rem = divmod(w.shape[0], n_grid)
        assert rem == 0 and rows % BF16_SUBLANE_TILE == 0, w.shape
        cast_specs.append(pl.BlockSpec((rows, w.shape[1]), grid_pos))
    kv_args, kv_specs, tiles = [], [], []
    for k, vt in sources:
        lk = k.shape[2]
        rows = min(tk, lk)
        tiles.append((rows, lk // rows))
        kv_args += [k, vt]
        assert sum(nk for _, nk in tiles) <= MAX_TILES_PER_HEAD, tiles
        kv_specs += [pl.BlockSpec((None, None, lk, HEAD_DIM), lambda bi, h, i: (bi, h, 0, 0)),
                     pl.BlockSpec((None, None, HEAD_DIM, lk), lambda bi, h, i: (bi, h, 0, 0))]
    out = pl.pallas_call(
        functools.partial(_flash_kernel, tiles=tuple(tiles), n_cast=len(cast)),
        grid=(b, N_KV_HEADS, nq),
        in_specs=([pl.BlockSpec((gw, tq), lambda bi, h, i: (h, bi * nq + i))] + kv_specs
                  + cast_specs),
        out_specs=[pl.BlockSpec((tq, gw), lambda bi, h, i: (bi * nq + i, h))] + cast_specs,
        out_shape=([jax.ShapeDtypeStruct((q_t.shape[1], q_t.shape[0]), BF16)]
                   + [jax.ShapeDtypeStruct(w.shape, BF16) for w in cast]),
        scratch_shapes=[pltpu.VMEM((GROUP, 1, tq), F32), pltpu.VMEM((GROUP, HEAD_DIM, tq), F32),
                        pltpu.VMEM((GROUP, 1, tq), F32), pltpu.VMEM((1, tq), F32)],
        compiler_params=pltpu.CompilerParams(
            dimension_semantics=("parallel", "parallel", "arbitrary"),
            vmem_limit_bytes=VMEM_LIMIT_BYTES),
        name="flash_attn",
    )(q_t, *kv_args, *cast)
    return out[0], out[1:]


def _mlp_kernel(*refs, with_proj, with_final_norm, ff_chunk):
    refs = list(refs)
    x_ref = refs.pop(0)
    if with_proj:
        o_ref, wo_ref, gate_mix_ref = refs.pop(0), refs.pop(0), refs.pop(0)
    g_ref, shift_ref, scale_ref, gate_ref, w1_ref, w2_ref = refs[:6]
    refs = refs[6:]
    if with_final_norm:
        fg_ref = refs.pop(0)
    (out_ref,) = refs

    x = x_ref[...]
    if with_proj:
        x = x + gate_mix_ref[...] * jnp.dot(o_ref[...], wo_ref[...], preferred_element_type=F32)
    h = _rms_mod(x, g_ref[...], shift_ref[...], scale_ref[...]).astype(BF16)
    y = jnp.zeros_like(x)
    for c in range(D_FF // ff_chunk):
        sl = slice(c * ff_chunk, (c + 1) * ff_chunk)
        a = jnp.maximum(jnp.dot(h, w1_ref[:, sl], preferred_element_type=F32), 0.0)
        y = y + jnp.dot((a * a).astype(BF16), w2_ref[sl, :], preferred_element_type=F32)
    x = x + gate_ref[...] * y
    if with_final_norm:
        x = x * lax.rsqrt(jnp.mean(x * x, axis=-1, keepdims=True) + EPS) * fg_ref[...]
    out_ref[...] = x


def _mlp(x2d, tm, mod, mod_tiles_per_row, norm_g, w1, w2, layer, *, attn_o=None, wo=None,
         final_g=None):
    t = x2d.shape[0]
    const = lambda i: (0, 0)
    this_layer = lambda i: (layer, 0, 0)
    tile = pl.BlockSpec((tm, D_MODEL), lambda i: (i, 0))
    args, specs = [x2d], [tile]
    if attn_o is not None:
        args += [attn_o, wo, mod]
        specs += [pl.BlockSpec((tm, Q_DIM), lambda i: (i, 0)), _resident((Q_DIM, D_MODEL), const),
                  _mod_spec(2, mod_tiles_per_row)]
    args += [norm_g, mod, mod, mod, w1, w2]
    specs += [pl.BlockSpec((1, D_MODEL), const), _mod_spec(3, mod_tiles_per_row),
              _mod_spec(4, mod_tiles_per_row), _mod_spec(5, mod_tiles_per_row),
              _resident((None, D_MODEL, D_FF), this_layer),
              _resident((None, D_FF, D_MODEL), this_layer)]
    if final_g is not None:
        args.append(final_g)
        specs.append(pl.BlockSpec((1, D_MODEL), const))
    return pl.pallas_call(
        functools.partial(_mlp_kernel, with_proj=attn_o is not None,
                          with_final_norm=final_g is not None, ff_chunk=1024),
        grid=(t // tm,),
        in_specs=specs,
        out_specs=tile,
        out_shape=jax.ShapeDtypeStruct(x2d.shape, F32),
        compiler_params=_params(1),
        name="mlp",
    )(*args)


def _gelu(z):
    return 0.5 * z * (1.0 + lax.erf(z * float(np.sqrt(0.5))))


SGU_COL_BLOCK = 2 * SGU_GROUP_DIM


def _sgu_kernel(x_ref, g_ref, shift_ref, scale_ref, gate_ref, win_ref, bin_ref, vg_ref, ws_ref,
                bs_ref, wout_ref, out_ref, v_scr, t_scr, zv_scr):
    tm = x_ref.shape[0]
    x = x_ref[...]
    h = _rms_mod(x, g_ref[...], shift_ref[...], scale_ref[...]).astype(BF16)
    sum_sq = jnp.zeros((tm, 1), F32)
    for cb in range(SGU_DIM // SGU_COL_BLOCK):
        cols = slice(cb * SGU_COL_BLOCK, (cb + 1) * SGU_COL_BLOCK)
        wcols = slice(SGU_DIM + cb * SGU_COL_BLOCK, SGU_DIM + (cb + 1) * SGU_COL_BLOCK)
        zb = _gelu(jnp.dot(h, win_ref[:, wcols], preferred_element_type=F32) + bin_ref[:, wcols])
        zv_scr[:, cols] = zb
        sum_sq = sum_sq + jnp.sum(zb * zb, axis=-1, keepdims=True)
    r = lax.rsqrt(sum_sq * (1.0 / SGU_DIM) + EPS)
    v_scr[...] = (zv_scr[...] * r * vg_ref[...]).astype(BF16)
    for cb in range(SGU_DIM // SGU_COL_BLOCK):
        cols = slice(cb * SGU_COL_BLOCK, (cb + 1) * SGU_COL_BLOCK)
        u = _gelu(jnp.dot(h, win_ref[:, cols], preferred_element_type=F32) + bin_ref[:, cols])
        for gi in range(SGU_COL_BLOCK // SGU_GROUP_DIM):
            grp = cb * (SGU_COL_BLOCK // SGU_GROUP_DIM) + gi
            w_s = ws_ref[grp]
            b_s = bs_ref[grp]
            for c in range(tm // CHUNK):
                rows = slice(c * CHUNK, (c + 1) * CHUNK)
                gcols = slice(grp * SGU_GROUP_DIM, (grp + 1) * SGU_GROUP_DIM)
                sv = jnp.dot(w_s, v_scr[rows, gcols], preferred_element_type=F32) + b_s
                lcols = slice(gi * SGU_GROUP_DIM, (gi + 1) * SGU_GROUP_DIM)
                t_scr[rows, gcols] = (u[rows, lcols] * sv).astype(BF16)
    y = jnp.dot(t_scr[...], wout_ref[...], preferred_element_type=F32)
    out_ref[...] = x + gate_ref[...] * y


def _sgu(x2d, tm, mod, mod_tiles_per_row, norm_g, w_in, b_in, v_g, w_s, b_s, w_out):
    t = x2d.shape[0]
    const = lambda i: (0, 0)
    const3 = lambda i: (0, 0, 0)
    tile = pl.BlockSpec((tm, D_MODEL), lambda i: (i, 0))
    return pl.pallas_call(
        _sgu_kernel,
        grid=(t // tm,),
        in_specs=[
            tile,
            pl.BlockSpec((1, D_MODEL), const),
            _mod_spec(0, mod_tiles_per_row),
            _mod_spec(1, mod_tiles_per_row),
            _mod_spec(2, mod_tiles_per_row),
            _resident((D_MODEL, 2 * SGU_DIM), const),
            pl.BlockSpec((1, 2 * SGU_DIM), const),
            pl.BlockSpec((1, SGU_DIM), const),
            pl.BlockSpec((SGU_GROUPS, CHUNK, CHUNK), const3),
            pl.BlockSpec((SGU_GROUPS, CHUNK, 1), const3),
            _resident((SGU_DIM, D_MODEL), const),
        ],
        out_specs=tile,
        out_shape=jax.ShapeDtypeStruct(x2d.shape, F32),
        scratch_shapes=[pltpu.VMEM((tm, SGU_DIM), BF16), pltpu.VMEM((tm, SGU_DIM), BF16),
                        pltpu.VMEM((tm, SGU_DIM), F32)],
        compiler_params=_params(1),
        name="sgu",
    )(x2d, norm_g, mod, mod, mod, w_in, b_in, v_g, w_s, b_s, w_out)


def _pairs_apart(w, n_heads):
    split = w.reshape(n_heads, HEAD_DIM // 2, 2, *w.shape[1:])
    return jnp.swapaxes(split, 1, 2).reshape(w.shape)


def kernel(x, c, ctx, c_ctx, ada_w, ada_b, mix_norm_g, mlp_norm_g, mlp_w1, mlp_w2, attn_wqkv,
           attn_q_g, attn_k_g, attn_wo, sgu_w_in, sgu_b_in, sgu_v_g, sgu_w_s, sgu_b_s, sgu_w_out,
           final_g):
    b, n, d = x.shape
    lc = ctx.shape[1]
    tm_x, tm_mlp, tm_sgu, tq, tk = QKV_TILE, MLP_TILE, SGU_TILE, FLASH_Q_TILE, FLASH_K_TILE
    tm_c = lc

    cond = jnp.concatenate([c, c_ctx[None, :], jnp.zeros((MOD_ROWS - b - 1, d), F32)], axis=0)
    mods = _adaln(cond, ada_w, ada_b)
    cos_t, sin_t = _rope_tables(n)
    cos_c = jnp.ones((HEAD_DIM // 2, lc), F32)
    sin_c = jnp.zeros((HEAD_DIM // 2, lc), F32)

    x2 = x.reshape(b * n, d)
    c2 = ctx.reshape(b * lc, d)
    row = lambda v: v.reshape(1, -1)
    stacked = dict(w1=mlp_w1, w2=mlp_w2, wo=attn_wo, w_in=sgu_w_in, w_out=sgu_w_out)
    bf16 = None
    for i in range(DEPTH):
        last = i == DEPTH - 1
        use_attn = (i % N_MIXERS) == 0
        j = i // N_MIXERS
        mod_x = mods[i, :b].reshape(b, 1, N_MOD * d)
        mod_c = mods[i, b:b + 1].reshape(1, 1, N_MOD * d)
        x_tpr, c_tpr = n // tm_x, (b * lc) // tm_c
        mix_g = row(mix_norm_g[i])
        mlp_g = row(mlp_norm_g[i])
        fg = row(final_g) if last else None
        if use_attn:
            w_t = attn_wqkv[j].T
            wqkv_t = jnp.concatenate(
                [_pairs_apart(w_t[:Q_DIM + KV_DIM], N_HEADS + N_KV_HEADS), w_t[Q_DIM + KV_DIM:]],
                axis=0).astype(BF16)
            q_g = _pairs_apart(attn_q_g[j], 1)[:, None]
            k_g = _pairs_apart(attn_k_g[j], 1)[:, None]
            q_t, k, vt = _qkv(x2, n, tm_x, mod_x, x_tpr, mix_g, wqkv_t, q_g, k_g, cos_t, sin_t)
            qc_t, kc, vtc = _qkv(c2, lc, tm_c, mod_c, c_tpr, mix_g, wqkv_t, q_g, k_g,
                                 cos_c, sin_c)
            to_cast = () if bf16 else tuple(w.reshape(-1, w.shape[-1]) for w in stacked.values())
            o, casted = _flash(q_t, [(k, vt), (kc, vtc)], n, tq, tk, cast=to_cast)
            if not bf16:
                bf16 = {name: wb.reshape(w.shape)
                        for (name, w), wb in zip(stacked.items(), casted)}
            w1, w2, wo = bf16["w1"], bf16["w2"], bf16["wo"][j]
            x2 = _mlp(x2, tm_mlp, mod_x, n // tm_mlp, mlp_g, w1, w2, i, attn_o=o, wo=wo,
                      final_g=fg)
            if not last:
                oc, _ = _flash(qc_t, [(kc, vtc)], lc, lc, tk)
                c2 = _mlp(c2, tm_c, mod_c, c_tpr, mlp_g, w1, w2, i, attn_o=oc, wo=wo)
        else:
            w1, w2 = bf16["w1"], bf16["w2"]
            sgu_args = (bf16["w_in"][j], row(sgu_b_in[j]), row(sgu_v_g[j]),
                        sgu_w_s[j].astype(BF16), sgu_b_s[j][:, :, None], bf16["w_out"][j])
            x2 = _sgu(x2, tm_sgu, mod_x, n // tm_sgu, mix_g, *sgu_args)
            x2 = _mlp(x2, tm_mlp, mod_x, n // tm_mlp, mlp_g, w1, w2, i, final_g=fg)
            if not last:
                c2 = _sgu(c2, tm_c, mod_c, c_tpr, mix_g, *sgu_args)
                c2 = _mlp(c2, tm_c, mod_c, c_tpr, mlp_g, w1, w2, i)
    return x2.reshape(b, n, d)
```

```python
import functools

import jax
import jax.numpy as jnp
import numpy as np
from jax import lax
from jax.experimental import pallas as pl
from jax.experimental.pallas import tpu as pltpu

D_MODEL = 1024
DEPTH = 2
GRID_W = 64
N_MIXERS = 2
N_HEADS = 8
N_KV_HEADS = 2
HEAD_DIM = 128
GROUP = N_HEADS // N_KV_HEADS
Q_DIM = N_HEADS * HEAD_DIM
KV_DIM = N_KV_HEADS * HEAD_DIM
QKV_DIM = Q_DIM + 2 * KV_DIM
ROPE_AXIS_DIM = HEAD_DIM // 2
ROPE_THETA = 10000.0
CHUNK = 128
SGU_DIM = 3 * D_MODEL
SGU_GROUPS = 8
SGU_GROUP_DIM = SGU_DIM // SGU_GROUPS
D_FF = 4 * D_MODEL
N_MOD = 6
EPS = 1e-6

F32 = jnp.float32
BF16 = jnp.bfloat16

VMEM_LIMIT_BYTES = 56 * 1024 * 1024
MOD_ROWS = 8

QKV_TILE = 1024
MLP_TILE = 1024
SGU_TILE = 4 * CHUNK
FLASH_Q_TILE = 2048
FLASH_K_TILE = 1024

SOFTMAX_Q_SCALE = float(HEAD_DIM ** -0.5 * np.log2(np.e))
NEG_BIG = -1e30
BF16_SUBLANE_TILE = 16


def _params(n_axes):
    return pltpu.CompilerParams(dimension_semantics=("parallel",) * n_axes,
                                vmem_limit_bytes=VMEM_LIMIT_BYTES)


def _resident(shape, index_map):
    return pl.BlockSpec(shape, index_map, pipeline_mode=pl.Buffered(1))


def _rms_mod(x, g, shift, scale):
    gain = g * (1.0 + scale)
    return x * lax.rsqrt(jnp.mean(x * x, axis=-1, keepdims=True) + EPS) * gain + shift


def _mod_spec(k, mod_row):
    return pl.BlockSpec((None, 1, D_MODEL), lambda i: (mod_row(i), 0, k))


def _adaln_kernel(c_ref, w_ref, b_ref, o_ref):
    c = c_ref[...]
    s = c * jax.nn.sigmoid(c)
    o_ref[...] = jnp.dot(s.astype(BF16), w_ref[...].astype(BF16),
                         preferred_element_type=F32) + b_ref[...]


def _adaln(cond, ada_w, ada_b):
    depth = ada_w.shape[0]
    return pl.pallas_call(
        _adaln_kernel,
        grid=(depth, N_MOD),
        in_specs=[
            pl.BlockSpec((MOD_ROWS, D_MODEL), lambda i, j: (0, 0)),
            pl.BlockSpec((None, D_MODEL, D_MODEL), lambda i, j: (i, 0, j)),
            pl.BlockSpec((None, 1, D_MODEL), lambda i, j: (i, 0, j)),
        ],
        out_specs=pl.BlockSpec((None, MOD_ROWS, D_MODEL), lambda i, j: (i, 0, j)),
        out_shape=jax.ShapeDtypeStruct((depth, MOD_ROWS, N_MOD * D_MODEL), F32),
        compiler_params=_params(2),
        name="adaln",
    )(cond, ada_w, ada_b.reshape(depth, 1, N_MOD * D_MODEL))


def _cos_sin_kernel(ang_ref, cos_ref, sin_ref):
    a = ang_ref[...]
    cos_ref[...] = jnp.cos(a)
    sin_ref[...] = jnp.sin(a)


def _rope_tables(n):
    rows_count = n // GRID_W
    freqs = 1.0 / (ROPE_THETA ** (jnp.arange(0, ROPE_AXIS_DIM, 2, dtype=F32) / ROPE_AXIS_DIM))
    pos = jnp.arange(rows_count + GRID_W, dtype=jnp.int32)
    pos = jnp.where(pos < rows_count, pos, pos - rows_count).astype(F32)
    ang = pos[:, None] * freqs[None, :]
    cos_s, sin_s = pl.pallas_call(
        _cos_sin_kernel,
        out_shape=(jax.ShapeDtypeStruct(ang.shape, F32),) * 2,
        name="rope_cos_sin",
    )(ang)

    def expand(t):
        by_row = jnp.repeat(t[:rows_count], GRID_W, axis=0)
        by_col = jnp.tile(t[rows_count:], (rows_count, 1))
        return jnp.concatenate([by_row, by_col], axis=-1)

    return expand(cos_s).T, expand(sin_s).T


Q_HEADS_PER_DOT = 4


def _qkv_kernel(x_ref, g_ref, shift_ref, scale_ref, wt_ref, qg_ref, kg_ref, cos_ref, sin_ref,
                qt_ref, k_ref, vt_ref):
    h = _rms_mod(x_ref[...], g_ref[...], shift_ref[...], scale_ref[...]).astype(BF16)
    cos = cos_ref[...]
    sin = sin_ref[...]
    half = HEAD_DIM // 2
    nt_dims = (((1,), (1,)), ((), ()))

    def norm_rope(t, g):
        t = t * lax.rsqrt(jnp.mean(t * t, axis=0, keepdims=True) + EPS) * g
        x1, x2 = t[:half], t[half:]
        return x1 * cos - x2 * sin, x1 * sin + x2 * cos

    qg = qg_ref[...] * SOFTMAX_Q_SCALE
    for blk in range(N_HEADS // Q_HEADS_PER_DOT):
        rows = slice(blk * Q_HEADS_PER_DOT * HEAD_DIM, (blk + 1) * Q_HEADS_PER_DOT * HEAD_DIM)
        yt = lax.dot_general(wt_ref[rows, :], h, nt_dims, preferred_element_type=F32)
        for j in range(Q_HEADS_PER_DOT):
            r0 = (blk * Q_HEADS_PER_DOT + j) * HEAD_DIM
            o1, o2 = norm_rope(yt[j * HEAD_DIM:(j + 1) * HEAD_DIM], qg)
            qt_ref[r0:r0 + half, :] = o1.astype(BF16)
            qt_ref[r0 + half:r0 + HEAD_DIM, :] = o2.astype(BF16)
    yt = lax.dot_general(wt_ref[Q_DIM:, :], h, nt_dims, preferred_element_type=F32)
    kg = kg_ref[...]
    for hd in range(N_KV_HEADS):
        o1, o2 = norm_rope(yt[hd * HEAD_DIM:(hd + 1) * HEAD_DIM], kg)
        k_ref[hd] = jnp.concatenate([o1, o2], axis=0).T.astype(BF16)
        v_t = yt[KV_DIM + hd * HEAD_DIM:KV_DIM + (hd + 1) * HEAD_DIM]
        vt_ref[hd] = v_t.astype(BF16)


def _qkv(x2d, seq, tm, mod, mod_row, norm_g, wqkv_t, q_g, k_g, cos_t, sin_t):
    t = x2d.shape[0]
    tpb = seq // tm
    const = lambda i: (0, 0)
    return pl.pallas_call(
        _qkv_kernel,
        grid=(t // tm,),
        in_specs=[
            pl.BlockSpec((tm, D_MODEL), lambda i: (i, 0)),
            pl.BlockSpec((1, D_MODEL), const),
            _mod_spec(0, mod_row),
            _mod_spec(1, mod_row),
            _resident((QKV_DIM, D_MODEL), const),
            pl.BlockSpec((HEAD_DIM, 1), const),
            pl.BlockSpec((HEAD_DIM, 1), const),
            pl.BlockSpec((HEAD_DIM // 2, tm), lambda i: (0, i % tpb)),
            pl.BlockSpec((HEAD_DIM // 2, tm), lambda i: (0, i % tpb)),
        ],
        out_specs=[
            pl.BlockSpec((Q_DIM, tm), lambda i: (0, i)),
            pl.BlockSpec((None, N_KV_HEADS, tm, HEAD_DIM), lambda i: (i // tpb, 0, i % tpb, 0)),
            pl.BlockSpec((None, N_KV_HEADS, HEAD_DIM, tm), lambda i: (i // tpb, 0, 0, i % tpb)),
        ],
        out_shape=[
            jax.ShapeDtypeStruct((Q_DIM, t), BF16),
            jax.ShapeDtypeStruct((t // seq, N_KV_HEADS, seq, HEAD_DIM), BF16),
            jax.ShapeDtypeStruct((t // seq, N_KV_HEADS, HEAD_DIM, seq), BF16),
        ],
        compiler_params=_params(1),
        name="qkv_proj",
    )(x2d, norm_g, mod, mod, wqkv_t, q_g, k_g, cos_t, sin_t)


MAX_TILES_PER_HEAD = 16
L_MIN = 2.0 ** -60


def _flash_kernel(*refs, tiles, n_cast):
    n_src = len(tiles)
    n_in = 1 + 2 * n_src + n_cast
    q_ref = refs[0]
    k_refs = refs[1:1 + 2 * n_src:2]
    vt_refs = refs[2:2 + 2 * n_src:2]
    o_ref = refs[n_in]
    shift_scr, acc_scr, l_scr, kmax_scr = refs[n_in + 1 + n_cast:]

    for w_ref, w_bf16_ref in zip(refs[n_in - n_cast:n_in], refs[n_in + 1:n_in + 1 + n_cast]):
        w_bf16_ref[...] = w_ref[...].astype(BF16)

    @pl.when(pl.program_id(2) == 0)
    def _():
        kmax2 = jnp.zeros((1, 1), F32)
        for k_ref, (tk, nk) in zip(k_refs, tiles):
            def body(t, n2, k_ref=k_ref, tk=tk):
                kk = k_ref[pl.ds(pl.multiple_of(t * tk, tk), tk), :].astype(F32)
                return jnp.maximum(n2, jnp.sum(kk * kk, axis=1, keepdims=True))

            n2 = lax.fori_loop(0, nk, body, jnp.zeros((tk, 1), F32))
            kmax2 = jnp.maximum(kmax2, jnp.max(n2, axis=0, keepdims=True))
        kmax_scr[...] = jnp.broadcast_to(jnp.sqrt(kmax2), kmax_scr.shape)

    def operands(src, g, t):
        tk = tiles[src][0]
        ks = t * tk if isinstance(t, int) else pl.multiple_of(t * tk, tk)
        qs = g * HEAD_DIM if isinstance(g, int) else pl.multiple_of(g * HEAD_DIM, HEAD_DIM)
        q_t = q_ref[pl.ds(qs, HEAD_DIM), :]
        return k_refs[src][pl.ds(ks, tk), :], q_t, vt_refs[src][:, pl.ds(ks, tk)]

    def finalize():
        l_min = None
        for g in range(GROUP):
            l = l_scr[g]
            o = acc_scr[g] * (1.0 / l)
            o_ref[:, g * HEAD_DIM:(g + 1) * HEAD_DIM] = o.astype(BF16).T
            l_min = l if l_min is None else jnp.minimum(l_min, l)
        return jnp.min(l_min)

    for g in range(GROUP):
        qf = q_ref[g * HEAD_DIM:(g + 1) * HEAD_DIM, :].astype(F32)
        shift_scr[g] = jnp.sqrt(jnp.sum(qf * qf, axis=0, keepdims=True)) * kmax_scr[...]

    def bounded_step(src, g, t):
        k_tile, q_t, vt_tile = operands(src, g, t)
        s = jnp.dot(k_tile, q_t, preferred_element_type=F32)
        p = jnp.exp2(s - shift_scr[g])
        l_tile = jnp.sum(p, axis=0, keepdims=True)
        pv = jnp.dot(vt_tile, p.astype(BF16), preferred_element_type=F32)
        if (src, t) == (0, 0):
            acc_scr[g], l_scr[g] = pv, l_tile
        else:
            acc_scr[g] += pv
            l_scr[g] += l_tile

    def per_head(step_fn):
        def body(g, carry):
            for src, (_, nk) in enumerate(tiles):
                for t in range(nk):
                    step_fn(src, g, t)
            return carry

        lax.fori_loop(0, GROUP, body, 0)

    per_head(bounded_step)
    l_min = finalize()

    @pl.when(jnp.logical_not(l_min >= L_MIN))
    def _():
        shift_scr[...] = jnp.full(shift_scr.shape, NEG_BIG, F32)
        acc_scr[...] = jnp.zeros(acc_scr.shape, F32)
        l_scr[...] = jnp.zeros(l_scr.shape, F32)

        def online_step(src, g, t):
            k_tile, q_t, vt_tile = operands(src, g, t)
            s = jnp.dot(k_tile, q_t, preferred_element_type=F32)
            m_old = shift_scr[g]
            m_new = jnp.maximum(m_old, jnp.max(s, axis=0, keepdims=True))
            shift_scr[g] = m_new
            alpha = jnp.exp2(m_old - m_new)
            p = jnp.exp2(s - m_new)
            l_scr[g] = alpha * l_scr[g] + jnp.sum(p, axis=0, keepdims=True)
            pv = jnp.dot(vt_tile, p.astype(BF16), preferred_element_type=F32)
            acc_scr[g] = alpha * acc_scr[g] + pv

        for src, (_, nk) in enumerate(tiles):
            def body(u, carry, src=src, nk=nk):
                g = u // nk
                online_step(src, g, u - g * nk)
                return carry

            lax.fori_loop(0, GROUP * nk, body, 0)
        finalize()


def _flash(q_t, sources, seq_q, tq, tk, cast=()):
    b = sources[0][0].shape[0]
    nq = seq_q // tq
    gw = GROUP * HEAD_DIM
    n_grid = b * N_KV_HEADS * nq
    grid_pos = lambda bi, h, i: ((bi * N_KV_HEADS + h) * nq + i, 0)
    cast_specs = []
    for w in cast:
        rows, rem = divmod(w.shape[0], n_grid)
        assert rem == 0 and rows % BF16_SUBLANE_TILE == 0, w.shape
        cast_specs.append(pl.BlockSpec((rows, w.shape[1]), grid_pos))
    kv_args, kv_specs, tiles = [], [], []
    for k, vt in sources:
        lk = k.shape[2]
        rows = min(tk, lk)
        tiles.append((rows, lk // rows))
        kv_args += [k, vt]
        assert sum(nk for _, nk in tiles) <= MAX_TILES_PER_HEAD, tiles
        kv_specs += [pl.BlockSpec((None, None, lk, HEAD_DIM), lambda bi, h, i: (bi, h, 0, 0)),
                     pl.BlockSpec((None, None, HEAD_DIM, lk), lambda bi, h, i: (bi, h, 0, 0))]
    out = pl.pallas_call(
        functools.partial(_flash_kernel, tiles=tuple(tiles), n_cast=len(cast)),
        grid=(b, N_KV_HEADS, nq),
        in_specs=([pl.BlockSpec((gw, tq), lambda bi, h, i: (h, bi * nq + i))] + kv_specs
                  + cast_specs),
        out_specs=[pl.BlockSpec((tq, gw), lambda bi, h, i: (bi * nq + i, h))] + cast_specs,
        out_shape=([jax.ShapeDtypeStruct((q_t.shape[1], q_t.shape[0]), BF16)]
                   + [jax.ShapeDtypeStruct(w.shape, BF16) for w in cast]),
        scratch_shapes=[pltpu.VMEM((GROUP, 1, tq), F32), pltpu.VMEM((GROUP, HEAD_DIM, tq), F32),
                        pltpu.VMEM((GROUP, 1, tq), F32), pltpu.VMEM((1, tq), F32)],
        compiler_params=pltpu.CompilerParams(
            dimension_semantics=("parallel", "parallel", "arbitrary"),
            vmem_limit_bytes=VMEM_LIMIT_BYTES),
        name="flash_attn",
    )(q_t, *kv_args, *cast)
    return out[0], out[1:]


def _mlp_kernel(*refs, with_proj, with_final_norm, ff_chunk):
    refs = list(refs)
    x_ref = refs.pop(0)
    if with_proj:
        o_ref, wo_ref, gate_mix_ref = refs.pop(0), refs.pop(0), refs.pop(0)
    g_ref, shift_ref, scale_ref, gate_ref, w1_ref, w2_ref = refs[:6]
    refs = refs[6:]
    if with_final_norm:
        fg_ref = refs.pop(0)
    (out_ref,) = refs

    x = x_ref[...]
    if with_proj:
        x = x + gate_mix_ref[...] * jnp.dot(o_ref[...], wo_ref[...], preferred_element_type=F32)
    h = _rms_mod(x, g_ref[...], shift_ref[...], scale_ref[...]).astype(BF16)
    y = jnp.zeros_like(x)
    for c in range(D_FF // ff_chunk):
        sl = slice(c * ff_chunk, (c + 1) * ff_chunk)
        a = jnp.maximum(jnp.dot(h, w1_ref[:, sl], preferred_element_type=F32), 0.0)
        y = y + jnp.dot((a * a).astype(BF16), w2_ref[sl, :], preferred_element_type=F32)
    x = x + gate_ref[...] * y
    if with_final_norm:
        x = x * lax.rsqrt(jnp.mean(x * x, axis=-1, keepdims=True) + EPS) * fg_ref[...]
    out_ref[...] = x


def _mlp(x2d, tm, mod, mod_row, norm_g, w1, w2, layer, *, attn_o=None, wo=None,
         final_g=None):
    t = x2d.shape[0]
    const = lambda i: (0, 0)
    this_layer = lambda i: (layer, 0, 0)
    tile = pl.BlockSpec((tm, D_MODEL), lambda i: (i, 0))
    args, specs = [x2d], [tile]
    if attn_o is not None:
        args += [attn_o, wo, mod]
        specs += [pl.BlockSpec((tm, Q_DIM), lambda i: (i, 0)), _resident((Q_DIM, D_MODEL), const),
                  _mod_spec(2, mod_row)]
    args += [norm_g, mod, mod, mod, w1, w2]
    specs += [pl.BlockSpec((1, D_MODEL), const), _mod_spec(3, mod_row),
              _mod_spec(4, mod_row), _mod_spec(5, mod_row),
              _resident((None, D_MODEL, D_FF), this_layer),
              _resident((None, D_FF, D_MODEL), this_layer)]
    if final_g is not None:
        args.append(final_g)
        specs.append(pl.BlockSpec((1, D_MODEL), const))
    return pl.pallas_call(
        functools.partial(_mlp_kernel, with_proj=attn_o is not None,
                          with_final_norm=final_g is not None, ff_chunk=1024),
        grid=(t // tm,),
        in_specs=specs,
        out_specs=tile,
        out_shape=jax.ShapeDtypeStruct(x2d.shape, F32),
        compiler_params=_params(1),
        name="mlp",
    )(*args)


def _gelu_x2(z):
    return z * (1.0 + lax.erf(z * float(np.sqrt(0.5))))


SGU_COL_BLOCK = 2 * SGU_GROUP_DIM


def _sgu_kernel(x_ref, g_ref, shift_ref, scale_ref, gate_ref, win_ref, bin_ref, vg_ref, ws_ref,
                bs_ref, wout_ref, out_ref, v_scr, t_scr, zv_scr):
    tm = x_ref.shape[0]
    x = x_ref[...]
    h = _rms_mod(x, g_ref[...], shift_ref[...], scale_ref[...]).astype(BF16)
    sum_sq = jnp.zeros((tm, 1), F32)
    for cb in range(SGU_DIM // SGU_COL_BLOCK):
        cols = slice(cb * SGU_COL_BLOCK, (cb + 1) * SGU_COL_BLOCK)
        wcols = slice(SGU_DIM + cb * SGU_COL_BLOCK, SGU_DIM + (cb + 1) * SGU_COL_BLOCK)
        zb = _gelu_x2(jnp.dot(h, win_ref[:, wcols], preferred_element_type=F32)
                      + bin_ref[:, wcols])
        zv_scr[:, cols] = zb
        sum_sq = sum_sq + jnp.sum(zb * zb, axis=-1, keepdims=True)
    r = lax.rsqrt(sum_sq * (1.0 / SGU_DIM) + 4.0 * EPS)
    v_scr[...] = (zv_scr[...] * r * vg_ref[...]).astype(BF16)
    for cb in range(SGU_DIM // SGU_COL_BLOCK):
        cols = slice(cb * SGU_COL_BLOCK, (cb + 1) * SGU_COL_BLOCK)
        u = _gelu_x2(jnp.dot(h, win_ref[:, cols], preferred_element_type=F32) + bin_ref[:, cols])
        for gi in range(SGU_COL_BLOCK // SGU_GROUP_DIM):
            grp = cb * (SGU_COL_BLOCK // SGU_GROUP_DIM) + gi
            w_s = ws_ref[grp]
            b_s = bs_ref[grp]
            for c in range(tm // CHUNK):
                rows = slice(c * CHUNK, (c + 1) * CHUNK)
                gcols = slice(grp * SGU_GROUP_DIM, (grp + 1) * SGU_GROUP_DIM)
                sv = jnp.dot(w_s, v_scr[rows, gcols], preferred_element_type=F32) + b_s
                lcols = slice(gi * SGU_GROUP_DIM, (gi + 1) * SGU_GROUP_DIM)
                t_scr[rows, gcols] = (u[rows, lcols] * sv).astype(BF16)
    y = jnp.dot(t_scr[...], wout_ref[...], preferred_element_type=F32)
    out_ref[...] = x + gate_ref[...] * y


def _sgu(x2d, tm, mod, mod_row, norm_g, w_in, b_in, v_g, w_s, b_s, w_out):
    t = x2d.shape[0]
    const = lambda i: (0, 0)
    const3 = lambda i: (0, 0, 0)
    tile = pl.BlockSpec((tm, D_MODEL), lambda i: (i, 0))
    return pl.pallas_call(
        _sgu_kernel,
        grid=(t // tm,),
        in_specs=[
            tile,
            pl.BlockSpec((1, D_MODEL), const),
            _mod_spec(0, mod_row),
            _mod_spec(1, mod_row),
            _mod_spec(2, mod_row),
            _resident((D_MODEL, 2 * SGU_DIM), const),
            pl.BlockSpec((1, 2 * SGU_DIM), const),
            pl.BlockSpec((1, SGU_DIM), const),
            pl.BlockSpec((SGU_GROUPS, CHUNK, CHUNK), const3),
            pl.BlockSpec((SGU_GROUPS, CHUNK, 1), const3),
            _resident((SGU_DIM, D_MODEL), const),
        ],
        out_specs=tile,
        out_shape=jax.ShapeDtypeStruct(x2d.shape, F32),
        scratch_shapes=[pltpu.VMEM((tm, SGU_DIM), BF16), pltpu.VMEM((tm, SGU_DIM), BF16),
                        pltpu.VMEM((tm, SGU_DIM), F32)],
        compiler_params=_params(1),
        name="sgu",
    )(x2d, norm_g, mod, mod, mod, w_in, b_in, v_g, w_s, b_s, w_out)


def _pairs_apart(w, n_heads):
    split = w.reshape(n_heads, HEAD_DIM // 2, 2, *w.shape[1:])
    return jnp.swapaxes(split, 1, 2).reshape(w.shape)


def kernel(x, c, ctx, c_ctx, ada_w, ada_b, mix_norm_g, mlp_norm_g, mlp_w1, mlp_w2, attn_wqkv,
           attn_q_g, attn_k_g, attn_wo, sgu_w_in, sgu_b_in, sgu_v_g, sgu_w_s, sgu_b_s, sgu_w_out,
           final_g):
    b, n, d = x.shape
    lc = ctx.shape[1]
    tm_x, tm_mlp, tm_sgu, tq, tk = QKV_TILE, MLP_TILE, SGU_TILE, FLASH_Q_TILE, FLASH_K_TILE
    tm_c = lc

    cond = jnp.concatenate([c, c_ctx[None, :], jnp.zeros((MOD_ROWS - b - 1, d), F32)], axis=0)
    mods = _adaln(cond, ada_w, ada_b).reshape(-1, 1, N_MOD * d)
    cos_t, sin_t = _rope_tables(n)
    cos_c = jnp.ones((HEAD_DIM // 2, lc), F32)
    sin_c = jnp.zeros((HEAD_DIM // 2, lc), F32)

    x2 = x.reshape(b * n, d)
    c2 = ctx.reshape(b * lc, d)
    row = lambda v: v.reshape(1, -1)
    stacked = dict(w1=mlp_w1, w2=mlp_w2, wo=attn_wo, w_in=sgu_w_in, w_out=sgu_w_out)
    bf16 = None
    for i in range(DEPTH):
        last = i == DEPTH - 1
        use_attn = (i % N_MIXERS) == 0
        j = i // N_MIXERS

        def x_row(tm, i=i):
            return lambda t: i * MOD_ROWS + t // (n // tm)

        c_row = lambda t, i=i: i * MOD_ROWS + b
        mix_g = row(mix_norm_g[i])
        mlp_g = row(mlp_norm_g[i])
        fg = row(final_g) if last else None
        if use_attn:
            w_t = attn_wqkv[j].T
            wqkv_t = jnp.concatenate(
                [_pairs_apart(w_t[:Q_DIM + KV_DIM], N_HEADS + N_KV_HEADS), w_t[Q_DIM + KV_DIM:]],
                axis=0).astype(BF16)
            q_g = _pairs_apart(attn_q_g[j], 1)[:, None]
            k_g = _pairs_apart(attn_k_g[j], 1)[:, None]
            q_t, k, vt = _qkv(x2, n, tm_x, mods, x_row(tm_x), mix_g, wqkv_t, q_g, k_g,
                              cos_t, sin_t)
            qc_t, kc, vtc = _qkv(c2, lc, tm_c, mods, c_row, mix_g, wqkv_t, q_g, k_g,
                                 cos_c, sin_c)
            to_cast = () if bf16 else tuple(w.reshape(-1, w.shape[-1]) for w in stacked.values())
            o, casted = _flash(q_t, [(k, vt), (kc, vtc)], n, tq, tk, cast=to_cast)
            if not bf16:
                bf16 = {name: wb.reshape(w.shape)
                        for (name, w), wb in zip(stacked.items(), casted)}
            w1, w2, wo = bf16["w1"], bf16["w2"], bf16["wo"][j]
            x2 = _mlp(x2, tm_mlp, mods, x_row(tm_mlp), mlp_g, w1, w2, i, attn_o=o, wo=wo,
                      final_g=fg)
            if not last:
                oc, _ = _flash(qc_t, [(kc, vtc)], lc, lc, tk)
                c2 = _mlp(c2, tm_c, mods, c_row, mlp_g, w1, w2, i, attn_o=oc, wo=wo)
        else:
            w1, w2 = bf16["w1"], bf16["w2"]
            sgu_args = (bf16["w_in"][j], row(sgu_b_in[j]), row(sgu_v_g[j]),
                        (0.5 * sgu_w_s[j]).astype(BF16), 0.5 * sgu_b_s[j][:, :, None],
                        bf16["w_out"][j])
            x2 = _sgu(x2, tm_sgu, mods, x_row(tm_sgu), mix_g, *sgu_args)
            x2 = _mlp(x2, tm_mlp, mods, x_row(tm_mlp), mlp_g, w1, w2, i, final_g=fg)
            if not last:
                c2 = _sgu(c2, tm_c, mods, c_row, mix_g, *sgu_args)
                c2 = _mlp(c2, tm_c, mods, c_row, mlp_g, w1, w2, i)
    return x2.reshape(b, n, d)
```
